```python
import math
import jax, jax.numpy as jnp
from jax import lax
import numpy as np

D_MODEL = 2048
BATCH = 2
SEQ = 8192
DEPTH = 4
DEC_BATCH = 4
DEC_SEQ = 2048
PAST_LEN = 128

N_MEM = 256
BLOCK = 128
EPS = 1e-6

MLA_HEADS = 4
MLA_Q_RANK = 384
MLA_KV_RANK = 256
MLA_NOPE = 128
MLA_ROPE = 64
MLA_V = 128
ROPE_BASE = 10000.0

SWA_HEADS = 8
SWA_KV_HEADS = 2
SWA_HEAD_DIM = 64
WINDOW = 128

SSD_HEADS = 16
SSD_HEAD_DIM = 64
SSD_GROUPS = 2
SSD_STATE = 128
SSD_CONV = 5
SSD_CHUNK = 128
SSD_INNER = SSD_HEADS * SSD_HEAD_DIM
SSD_XBC = SSD_INNER + 2 * SSD_GROUPS * SSD_STATE

MIX_WIDTH = MLA_HEADS * MLA_V + SWA_HEADS * SWA_HEAD_DIM + SSD_INNER

IN_SIZES = (MLA_Q_RANK, MLA_KV_RANK, MLA_ROPE,
            SWA_HEADS * SWA_HEAD_DIM, SWA_KV_HEADS * SWA_HEAD_DIM, SWA_KV_HEADS * SWA_HEAD_DIM,
            SSD_INNER, SSD_XBC, 2 * SSD_HEADS)
IN_WIDTH = (MLA_Q_RANK + MLA_KV_RANK + MLA_ROPE + SWA_HEADS * SWA_HEAD_DIM
            + 2 * SWA_KV_HEADS * SWA_HEAD_DIM + SSD_INNER + SSD_XBC + 2 * SSD_HEADS)

X_HEADS = 4
X_HEAD_DIM = 128

D_FF = 5632
FFN_CONV = 3

kernel_name = 'hybrid_bidir_encoder_mla_swa_ssd'


def rms_norm(x, w):
    xf = x.astype(jnp.float32)
    y = xf * lax.rsqrt(jnp.mean(xf * xf, axis=-1, keepdims=True) + EPS)
    return (y * w.astype(jnp.float32)).astype(x.dtype)


def rope(x, pos):
    half = x.shape[-1] // 2
    inv = ROPE_BASE ** (-jnp.arange(half, dtype=jnp.float32) / half)
    ang = pos.astype(jnp.float32)[:, None] * inv[None, :]
    cos, sin = jnp.cos(ang)[:, None, :], jnp.sin(ang)[:, None, :]
    xf = x.astype(jnp.float32)
    x1, x2 = xf[..., :half], xf[..., half:]
    return jnp.concatenate([x1 * cos - x2 * sin, x2 * cos + x1 * sin], axis=-1).astype(x.dtype)


def depthwise_conv(x, w, b):
    width = w.shape[0]
    r = width // 2
    s = x.shape[1]
    xp = jnp.pad(x, ((0, 0), (r, r), (0, 0)))
    out = b
    for i in range(width):
        out = out + xp[:, i:i + s] * w[i]
    return out


def alibi_slopes(n):
    return 2.0 ** (-8.0 * jnp.arange(1, n + 1, dtype=jnp.float32) / n)


def mla_attention(cq, ckv, kr, w_uq, w_ukv, q_norm, kv_norm):
    b, s, _ = cq.shape
    pos = jnp.arange(s)
    q = (rms_norm(cq, q_norm) @ w_uq).reshape(b, s, MLA_HEADS, MLA_NOPE + MLA_ROPE)
    kv = (rms_norm(ckv, kv_norm) @ w_ukv).reshape(b, s, MLA_HEADS, MLA_NOPE + MLA_V)
    q_nope, q_rope = q[..., :MLA_NOPE], rope(q[..., MLA_NOPE:], pos)
    k_nope, v = kv[..., :MLA_NOPE], kv[..., MLA_NOPE:]
    k_rope = rope(kr[:, :, None, :], pos)[:, :, 0]
    scale = (MLA_NOPE + MLA_ROPE) ** -0.5
    nb = s // BLOCK
    qn_blocks = q_nope.reshape(b, nb, BLOCK, MLA_HEADS, MLA_NOPE).transpose(1, 0, 2, 3, 4)
    qr_blocks = q_rope.reshape(b, nb, BLOCK, MLA_HEADS, MLA_ROPE).transpose(1, 0, 2, 3, 4)

    def query_block(args):
        qn, qr = args
        sc = (jnp.einsum('bqhd,bkhd->bhqk', qn, k_nope)
              + jnp.einsum('bqhd,bkd->bhqk', qr, k_rope)).astype(jnp.float32) * scale
        p = jax.nn.softmax(sc, axis=-1).astype(v.dtype)
        return jnp.einsum('bhqk,bkhd->bqhd', p, v)

    o = lax.map(query_block, (qn_blocks, qr_blocks))
    return o.transpose(1, 0, 2, 3, 4).reshape(b, s, MLA_HEADS * MLA_V)


def window_attention(q, k, v, sink):
    b, s, _ = q.shape
    nb = s // BLOCK
    grp = SWA_HEADS // SWA_KV_HEADS
    q = q.reshape(b, nb, BLOCK, SWA_KV_HEADS, grp, SWA_HEAD_DIM)

    def neighbours(t):
        t = t.reshape(b, s, SWA_KV_HEADS, SWA_HEAD_DIM)
        tp = jnp.pad(t, ((0, 0), (BLOCK, BLOCK), (0, 0), (0, 0)))
        tp = tp.reshape(b, nb + 2, BLOCK, SWA_KV_HEADS, SWA_HEAD_DIM)
        return jnp.concatenate([tp[:, :-2], tp[:, 1:-1], tp[:, 2:]], axis=2)

    kb, vb = neighbours(k), neighbours(v)
    sc = jnp.einsum('bnqkgd,bnjkd->bnkgqj', q, kb).astype(jnp.float32) * (SWA_HEAD_DIM ** -0.5)
    qi = jnp.arange(BLOCK)[:, None]
    kj = jnp.arange(3 * BLOCK)[None, :]
    dist = jnp.abs(qi + BLOCK - kj)
    kpos = jnp.arange(nb)[:, None] * BLOCK - BLOCK + jnp.arange(3 * BLOCK)[None, :]
    valid = (dist <= WINDOW)[None] & ((kpos >= 0) & (kpos < s))[:, None, :]
    slopes = alibi_slopes(SWA_HEADS).reshape(SWA_KV_HEADS, grp)
    sc = sc - slopes[:, :, None, None] * dist.astype(jnp.float32)
    sc = jnp.where(valid[None, :, None, None], sc, -jnp.inf)
    sink_l = sink.astype(jnp.float32).reshape(SWA_KV_HEADS, grp)[:, :, None, None]
    m = jnp.maximum(jnp.max(sc, axis=-1, keepdims=True), sink_l)
    p = jnp.exp(sc - m)
    denom = jnp.sum(p, axis=-1, keepdims=True) + jnp.exp(sink_l - m)
    o = jnp.einsum('bnkgqj,bnjkd->bnqkgd', (p / denom).astype(v.dtype), vb)
    return o.reshape(b, s, SWA_HEADS * SWA_HEAD_DIM)


def ssd_scan(x, dt, a_head, bm, cm):
    b, s, h, p = x.shape
    n = bm.shape[-1]
    nc, L = s // SSD_CHUNK, SSD_CHUNK
    xdt = (x.astype(jnp.float32) * dt[..., None]).reshape(b, nc, L, h, p)
    bc = bm.astype(jnp.float32).reshape(b, nc, L, h, n)
    cc = cm.astype(jnp.float32).reshape(b, nc, L, h, n)
    a = (dt * a_head).reshape(b, nc, L, h).transpose(0, 3, 1, 2)
    a_cum = jnp.cumsum(a, axis=-1)
    seg = a_cum[..., :, None] - a_cum[..., None, :]
    lower = jnp.tril(jnp.ones((L, L), dtype=bool))
    decay_in = jnp.exp(jnp.where(lower, seg, -jnp.inf))
    scores = jnp.einsum('bclhn,bcshn->bhcls', cc, bc) * decay_in
    y_diag = jnp.einsum('bhcls,bcshp->bclhp', scores, xdt)
    decay_states = jnp.exp(a_cum[..., -1:] - a_cum).transpose(0, 2, 3, 1)
    states = jnp.einsum('bclhn,bclhp->bchpn', bc * decay_states[..., None], xdt)
    chunk_decay = jnp.exp(a_cum[..., -1]).transpose(2, 0, 1)

    def step(carry, inp):
        st, dec = inp
        return carry * dec[..., None, None] + st, carry

    _, prev = lax.scan(step, jnp.zeros((b, h, p, n), jnp.float32),
                       (states.transpose(1, 0, 2, 3, 4), chunk_decay))
    prev = prev.transpose(1, 0, 2, 3, 4)
    y_off = jnp.einsum('bclhn,bchpn->bclhp', cc, prev) * jnp.exp(a_cum).transpose(0, 2, 3, 1)[..., None]
    return (y_diag + y_off).reshape(b, s, h, p)


def ssd_mixer(z, xbc, dt_raw, conv_w, conv_b, a_log, dt_bias, d_skip, norm_w):
    b, s, _ = z.shape
    xbc = jax.nn.silu(depthwise_conv(xbc, conv_w, conv_b))
    xs, bm, cm = jnp.split(xbc, [SSD_INNER, SSD_INNER + SSD_GROUPS * SSD_STATE], axis=-1)
    xs = xs.reshape(b, s, SSD_HEADS, SSD_HEAD_DIM)
    rep = SSD_HEADS // SSD_GROUPS
    bm = jnp.repeat(bm.reshape(b, s, SSD_GROUPS, SSD_STATE), rep, axis=2)
    cm = jnp.repeat(cm.reshape(b, s, SSD_GROUPS, SSD_STATE), rep, axis=2)
    dt = jax.nn.softplus(dt_raw.astype(jnp.float32).reshape(b, s, 2, SSD_HEADS)
                         + dt_bias.astype(jnp.float32))
    a_head = -jnp.exp(a_log.astype(jnp.float32))
    flip = lambda t: jnp.flip(t, axis=1)
    y_fwd = ssd_scan(xs, dt[:, :, 0], a_head[0], bm, cm)
    y_bwd = flip(ssd_scan(flip(xs), flip(dt[:, :, 1]), a_head[1], flip(bm), flip(cm)))
    y = y_fwd + y_bwd + xs.astype(jnp.float32) * d_skip.astype(jnp.float32)[:, None]
    y = y.reshape(b, s, SSD_INNER) * jax.nn.silu(z.astype(jnp.float32))
    yg = y.reshape(b, s, SSD_GROUPS, SSD_INNER // SSD_GROUPS)
    yg = yg * lax.rsqrt(jnp.mean(yg * yg, axis=-1, keepdims=True) + EPS)
    return (yg.reshape(b, s, SSD_INNER) * norm_w.astype(jnp.float32)).astype(z.dtype)


def parallel_mixer(h, p):
    proj = h @ p['w_in']
    cuts = [int(c) for c in np.cumsum(IN_SIZES)[:-1]]
    cq, ckv, kr, q_s, k_s, v_s, z, xbc, dt_raw = jnp.split(proj, cuts, axis=-1)
    o_a = mla_attention(cq, ckv, kr, p['mla_w_uq'], p['mla_w_ukv'], p['mla_q_norm'], p['mla_kv_norm'])
    o_b = window_attention(q_s, k_s, v_s, p['swa_sink'])
    o_c = ssd_mixer(z, xbc, dt_raw, p['ssd_conv_w'], p['ssd_conv_b'], p['ssd_a_log'],
                    p['ssd_dt_bias'], p['ssd_d'], p['ssd_norm'])
    return jnp.concatenate([o_a, o_b, o_c], axis=-1) @ p['w_out']


def memory_attention(h, mem, p):
    b, s, _ = h.shape
    n_mem = mem.shape[1]
    m = rms_norm(mem, p['norm_mem_kv'])
    q = (h @ p['xattn_wq']).reshape(b, s, X_HEADS, X_HEAD_DIM)
    kv = (m @ p['xattn_wkv']).reshape(b, n_mem, 2, X_HEADS, X_HEAD_DIM)
    k, v = kv[:, :, 0], kv[:, :, 1]
    sc = jnp.einsum('bqhd,bkhd->bhqk', q, k).astype(jnp.float32) * (X_HEAD_DIM ** -0.5)
    pr = jax.nn.softmax(sc, axis=-1).astype(v.dtype)
    o = jnp.einsum('bhqk,bkhd->bqhd', pr, v).reshape(b, s, X_HEADS * X_HEAD_DIM)
    return o @ p['xattn_wo']


def conv_ffn(h, p):
    g, u = jnp.split(h @ p['ffn_w_up'], 2, axis=-1)
    a = jax.nn.silu(depthwise_conv(g, p['ffn_conv_w'], p['ffn_conv_b'])) * u
    return a @ p['ffn_w_down']


def encoder_trunk(x, mem, params):
    for layer in range(DEPTH):
        p = {name: arr[layer] for name, arr in params.items()}
        x = x + rms_norm(parallel_mixer(rms_norm(x, p['norm_mix_pre']), p), p['norm_mix_post'])
        x = x + rms_norm(memory_attention(rms_norm(x, p['norm_mem_pre']), mem, p), p['norm_mem_post'])
        x = x + rms_norm(conv_ffn(rms_norm(x, p['norm_ffn_pre']), p), p['norm_ffn_post'])
    return x


def setup_inputs(seed: int = 0) -> dict:
    key = jax.random.key(seed)
    ks = jax.random.split(key, 32)
    f32 = jnp.float32

    def nrm(k, shape, scale):
        return jax.random.normal(k, shape, f32) * scale

    def gain(k, width):
        return 1.0 + 0.02 * jax.random.normal(k, (DEPTH, width), f32)

    dt0 = jnp.exp(jax.random.uniform(ks[13], (DEPTH, 2, SSD_HEADS), f32,
                                     math.log(1e-3), math.log(1e-1)))
    return {
        'x_prompt': nrm(ks[0], (BATCH, SEQ, D_MODEL), 1.0),
        'x_sample': nrm(ks[1], (DEC_BATCH, DEC_SEQ, D_MODEL), 1.0),
        'mem_prompt': nrm(ks[2], (BATCH, N_MEM, D_MODEL), 1.0),
        'mem_sample': nrm(ks[3], (DEC_BATCH, N_MEM, D_MODEL), 1.0),
        'norm_mix_pre': gain(ks[4], D_MODEL),
        'norm_mix_post': gain(ks[5], D_MODEL),
        'w_in': nrm(ks[6], (DEPTH, D_MODEL, IN_WIDTH), D_MODEL ** -0.5),
        'mla_q_norm': gain(ks[7], MLA_Q_RANK),
        'mla_kv_norm': gain(ks[8], MLA_KV_RANK),
        'mla_w_uq': nrm(ks[9], (DEPTH, MLA_Q_RANK, MLA_HEADS * (MLA_NOPE + MLA_ROPE)), MLA_Q_RANK ** -0.5),
        'mla_w_ukv': nrm(ks[10], (DEPTH, MLA_KV_RANK, MLA_HEADS * (MLA_NOPE + MLA_V)), MLA_KV_RANK ** -0.5),
        'swa_sink': nrm(ks[11], (DEPTH, SWA_HEADS), 0.5),
        'ssd_conv_w': nrm(ks[12], (DEPTH, SSD_CONV, SSD_XBC), SSD_CONV ** -0.5),
        'ssd_conv_b': nrm(ks[14], (DEPTH, SSD_XBC), 0.01),
        'ssd_a_log': jnp.log(jax.random.uniform(ks[15], (DEPTH, 2, SSD_HEADS), f32, 1.0, 16.0)),
        'ssd_dt_bias': dt0 + jnp.log(-jnp.expm1(-dt0)),
        'ssd_d': 1.0 + 0.1 * jax.random.normal(ks[16], (DEPTH, SSD_HEADS), f32),
        'ssd_norm': gain(ks[17], SSD_INNER),
        'w_out': nrm(ks[18], (DEPTH, MIX_WIDTH, D_MODEL), MIX_WIDTH ** -0.5),
        'norm_mem_pre': gain(ks[19], D_MODEL),
        'norm_mem_post': gain(ks[20], D_MODEL),
        'norm_mem_kv': gain(ks[21], D_MODEL),
        'xattn_wq': nrm(ks[22], (DEPTH, D_MODEL, X_HEADS * X_HEAD_DIM), D_MODEL ** -0.5),
        'xattn_wkv': nrm(ks[23], (DEPTH, D_MODEL, 2 * X_HEADS * X_HEAD_DIM), D_MODEL ** -0.5),
        'xattn_wo': nrm(ks[24], (DEPTH, X_HEADS * X_HEAD_DIM, D_MODEL), (X_HEADS * X_HEAD_DIM) ** -0.5),
        'norm_ffn_pre': gain(ks[25], D_MODEL),
        'norm_ffn_post': gain(ks[26], D_MODEL),
        'ffn_w_up': nrm(ks[27], (DEPTH, D_MODEL, 2 * D_FF), D_MODEL ** -0.5),
        'ffn_conv_w': nrm(ks[28], (DEPTH, FFN_CONV, D_FF), FFN_CONV ** -0.5),
        'ffn_conv_b': nrm(ks[29], (DEPTH, D_FF), 0.01),
        'ffn_w_down': nrm(ks[30], (DEPTH, D_FF, D_MODEL), D_FF ** -0.5),
    }


def reference(x_prompt, x_sample, mem_prompt, mem_sample,
              norm_mix_pre, norm_mix_post, w_in, mla_q_norm, mla_kv_norm, mla_w_uq, mla_w_ukv,
              swa_sink, ssd_conv_w, ssd_conv_b, ssd_a_log, ssd_dt_bias, ssd_d, ssd_norm, w_out,
              norm_mem_pre, norm_mem_post, norm_mem_kv, xattn_wq, xattn_wkv, xattn_wo,
              norm_ffn_pre, norm_ffn_post, ffn_w_up, ffn_conv_w, ffn_conv_b, ffn_w_down):
    params = dict(
        norm_mix_pre=norm_mix_pre, norm_mix_post=norm_mix_post, w_in=w_in,
        mla_q_norm=mla_q_norm, mla_kv_norm=mla_kv_norm, mla_w_uq=mla_w_uq, mla_w_ukv=mla_w_ukv,
        swa_sink=swa_sink, ssd_conv_w=ssd_conv_w, ssd_conv_b=ssd_conv_b, ssd_a_log=ssd_a_log,
        ssd_dt_bias=ssd_dt_bias, ssd_d=ssd_d, ssd_norm=ssd_norm, w_out=w_out,
        norm_mem_pre=norm_mem_pre, norm_mem_post=norm_mem_post, norm_mem_kv=norm_mem_kv,
        xattn_wq=xattn_wq, xattn_wkv=xattn_wkv, xattn_wo=xattn_wo,
        norm_ffn_pre=norm_ffn_pre, norm_ffn_post=norm_ffn_post, ffn_w_up=ffn_w_up,
        ffn_conv_w=ffn_conv_w, ffn_conv_b=ffn_conv_b, ffn_w_down=ffn_w_down)
    y_prompt = encoder_trunk(x_prompt, mem_prompt, params)
    y_sample = encoder_trunk(x_sample, mem_sample, params)
    return (y_prompt, y_sample)
```

```python
import functools
import math

import numpy as np
import jax
import jax.numpy as jnp
from jax import lax
from jax.experimental import pallas as pl
from jax.experimental.pallas import tpu as pltpu

F32 = jnp.float32
BF16 = jnp.bfloat16

D_MODEL = 2048
DEPTH = 4
N_MEM = 256
BLOCK = 128
EPS = 1e-6

MLA_HEADS = 4
MLA_Q_RANK = 384
MLA_KV_RANK = 256
MLA_NOPE = 128
MLA_ROPE = 64
MLA_V = 128
ROPE_BASE = 10000.0
MLA_QK = 2 * MLA_NOPE

SWA_HEADS = 8
SWA_KV_HEADS = 2
SWA_HEAD_DIM = 64
SWA_Q = SWA_HEADS * SWA_HEAD_DIM
SWA_KV = SWA_KV_HEADS * SWA_HEAD_DIM
WINDOW = 128

SSD_HEADS = 16
SSD_HEAD_DIM = 64
SSD_GROUPS = 2
SSD_STATE = 128
SSD_CONV = 5
SSD_CHUNK = 128
SSD_INNER = SSD_HEADS * SSD_HEAD_DIM
SSD_XBC = SSD_INNER + 2 * SSD_GROUPS * SSD_STATE
SSD_GROUP_W = SSD_INNER // SSD_GROUPS
SSD_PAIRS = SSD_HEADS // 2

X_HEADS = 4
X_HEAD_DIM = 128
X_WIDTH = X_HEADS * X_HEAD_DIM

D_FF = 5632
FFN_CONV = 3

LANES = 128
BF16_ROWS = 16
VMEM_MIB = 2 ** 20

PROJ_XBC = 0
PROJ_MLA = SSD_XBC
MLA_SLAB = MLA_Q_RANK + MLA_KV_RANK + 2 * MLA_ROPE
PROJ_SWA = PROJ_MLA + MLA_SLAB
SWA_SLAB = SWA_Q + 2 * SWA_KV
PROJ_Z = PROJ_SWA + SWA_SLAB
PROJ_W = PROJ_Z + SSD_INNER
DT_W = LANES

NT_DIMS = (((1,), (1,)), ((), ()))


def _cparams(semantics, vmem_mib):
    return pltpu.CompilerParams(dimension_semantics=semantics, vmem_limit_bytes=vmem_mib * VMEM_MIB)


def _rms(x, g):
    ms = jnp.mean(x * x, axis=-1, keepdims=True)
    return x * lax.rsqrt(ms + EPS) * g


def _silu(x):
    return x * jax.nn.sigmoid(x)


def _softplus(x):
    return jnp.maximum(x, 0.0) + jnp.log1p(jnp.exp(-jnp.abs(x)))


def _tile(n, pref):
    return pref if n % pref == 0 else n


def _in_proj_kernel(x_ref, g_ref, w_ref, wdt_ref, o_ref, dt_ref, h_scr):
    @pl.when(pl.program_id(2) == 0)
    def _():
        h = _rms(x_ref[0], g_ref[...]).astype(BF16)
        h_scr[...] = h
        dt_ref[0] = jnp.dot(h, wdt_ref[...], preferred_element_type=F32)

    o_ref[0] = jnp.dot(h_scr[...], w_ref[...], preferred_element_type=F32).astype(BF16)


def _in_proj(x, g, w, wdt):
    b, s, d = x.shape
    tm = _tile(s, 1024)
    tn = 1024
    return pl.pallas_call(
        _in_proj_kernel,
        grid=(b, s // tm, PROJ_W // tn),
        in_specs=[
            pl.BlockSpec((1, tm, d), lambda bi, i, j: (bi, i, 0)),
            pl.BlockSpec((1, d), lambda bi, i, j: (0, 0)),
            pl.BlockSpec((d, tn), lambda bi, i, j: (0, j)),
            pl.BlockSpec((d, DT_W), lambda bi, i, j: (0, 0)),
        ],
        out_specs=[
            pl.BlockSpec((1, tm, tn), lambda bi, i, j: (bi, i, j)),
            pl.BlockSpec((1, tm, DT_W), lambda bi, i, j: (bi, i, 0)),
        ],
        out_shape=[
            jax.ShapeDtypeStruct((b, s, PROJ_W), BF16),
            jax.ShapeDtypeStruct((b, s, DT_W), F32),
        ],
        scratch_shapes=[pltpu.VMEM((tm, d), BF16)],
        compiler_params=_cparams(("parallel", "parallel", "arbitrary"), 48),
        name="in_proj",
    )(x, g, w, wdt)


def _mla_prep_kernel(a_ref, cs_ref, qn_ref, kvn_ref, wuq_ref, wukv_ref, q_ref, k_ref, v_ref):
    a = a_ref[0].astype(F32)
    cq = a[:, :MLA_Q_RANK]
    ckv = a[:, MLA_Q_RANK:MLA_Q_RANK + MLA_KV_RANK]
    kr = a[:, MLA_Q_RANK + MLA_KV_RANK:]
    cs = cs_ref[...]
    scale = (MLA_NOPE + MLA_ROPE) ** -0.5
    q = jnp.dot(_rms(cq, qn_ref[...]).astype(BF16), wuq_ref[...], preferred_element_type=F32) * scale
    kv = jnp.dot(_rms(ckv, kvn_ref[...]).astype(BF16), wukv_ref[...], preferred_element_type=F32)
    t = kr * cs
    k_rope = (t + pltpu.roll(t, MLA_ROPE, 1)).astype(BF16)
    for h in range(MLA_HEADS):
        lo = h * MLA_QK
        mid = lo + MLA_NOPE
        hi = lo + MLA_QK
        q_ref[0, :, lo:mid] = q[:, lo:mid].astype(BF16)
        q_ref[0, :, mid:hi] = (q[:, mid:hi] * cs).astype(BF16)
        k_ref[0, :, lo:mid] = kv[:, h * MLA_NOPE:(h + 1) * MLA_NOPE].astype(BF16)
        k_ref[0, :, mid:hi] = k_rope
    v_ref[0] = kv[:, MLA_HEADS * MLA_NOPE:].astype(BF16)


def _mla_prep(proj, cs, qn, kvn, wuq, wukv):
    b, s, _ = proj.shape
    tm = _tile(s, 512)
    const = lambda bi, i: (0, 0)
    return pl.pallas_call(
        _mla_prep_kernel,
        grid=(b, s // tm),
        in_specs=[
            pl.BlockSpec((1, tm, MLA_SLAB), lambda bi, i: (bi, i, PROJ_MLA // MLA_SLAB)),
            pl.BlockSpec((tm, LANES), lambda bi, i: (i, 0)),
            pl.BlockSpec((1, MLA_Q_RANK), const),
            pl.BlockSpec((1, MLA_KV_RANK), const),
            pl.BlockSpec((MLA_Q_RANK, MLA_HEADS * MLA_QK), const),
            pl.BlockSpec((MLA_KV_RANK, MLA_HEADS * (MLA_NOPE + MLA_V)), const),
        ],
        out_specs=[
            pl.BlockSpec((1, tm, MLA_HEADS * MLA_QK), lambda bi, i: (bi, i, 0)),
            pl.BlockSpec((1, tm, MLA_HEADS * MLA_QK), lambda bi, i: (bi, i, 0)),
            pl.BlockSpec((1, tm, MLA_HEADS * MLA_V), lambda bi, i: (bi, i, 0)),
        ],
        out_shape=[
            jax.ShapeDtypeStruct((b, s, MLA_HEADS * MLA_QK), BF16),
            jax.ShapeDtypeStruct((b, s, MLA_HEADS * MLA_QK), BF16),
            jax.ShapeDtypeStruct((b, s, MLA_HEADS * MLA_V), BF16),
        ],
        compiler_params=_cparams(("parallel", "parallel"), 32),
        name="mla_prep",
    )(proj, cs, qn, kvn, wuq, wukv)


def _mla_flash_kernel(q_ref, k_ref, v_ref, o_ref, m_scr, l_scr, acc_scr, *, tk):
    q = q_ref[0]
    m_scr[...] = jnp.full(m_scr.shape, -jnp.inf, F32)
    l_scr[...] = jnp.zeros(l_scr.shape, F32)
    acc_scr[...] = jnp.zeros(acc_scr.shape, F32)

    def body(kc, carry):
        off = pl.multiple_of(kc * tk, tk)
        k = k_ref[0, pl.ds(off, tk), :]
        v = v_ref[0, pl.ds(off, tk), :]
        s = lax.dot_general(q, k, NT_DIMS, preferred_element_type=F32)
        m_prev = m_scr[...]
        m_new = jnp.maximum(m_prev, jnp.max(s, axis=-1, keepdims=True))
        alpha = jnp.exp(m_prev - m_new)
        p = jnp.exp(s - m_new)
        l_scr[...] = alpha * l_scr[...] + jnp.sum(p, axis=-1, keepdims=True)
        acc_scr[...] = alpha * acc_scr[...] + jnp.dot(p.astype(BF16), v, preferred_element_type=F32)
        m_scr[...] = m_new
        return carry

    lax.fori_loop(0, k_ref.shape[1] // tk, body, 0)
    o_ref[0] = (acc_scr[...] * (1.0 / l_scr[...])).astype(BF16)


def _mla_flash(q, k, v):
    b, s, _ = q.shape
    tq = _tile(s, 512)
    tk = _tile(s, 512)
    return pl.pallas_call(
        functools.partial(_mla_flash_kernel, tk=tk),
        grid=(b, MLA_HEADS, s // tq),
        in_specs=[
            pl.BlockSpec((1, tq, MLA_QK), lambda bi, h, i: (bi, i, h)),
            pl.BlockSpec((1, s, MLA_QK), lambda bi, h, i: (bi, 0, h)),
            pl.BlockSpec((1, s, MLA_V), lambda bi, h, i: (bi, 0, h)),
        ],
        out_specs=pl.BlockSpec((1, tq, MLA_V), lambda bi, h, i: (bi, i, h)),
        out_shape=jax.ShapeDtypeStruct((b, s, MLA_HEADS * MLA_V), BF16),
        scratch_shapes=[
            pltpu.VMEM((tq, 1), F32),
            pltpu.VMEM((tq, 1), F32),
            pltpu.VMEM((tq, MLA_V), F32),
        ],
        compiler_params=_cparams(("parallel", "parallel", "arbitrary"), 40),
        name="mla_flash",
    )(q, k, v)


def _swa_kernel(sink_ref, main_ref, prev_ref, next_ref, o_ref, *, seq_len, tq):
    i = pl.program_id(1)
    main = main_ref[0]
    kv = jnp.concatenate([prev_ref[0], main[:, SWA_Q:], next_ref[0]], axis=0)
    nkeys = 3 * BLOCK
    qi = lax.broadcasted_iota(jnp.int32, (BLOCK, nkeys), 0)
    kj = lax.broadcasted_iota(jnp.int32, (BLOCK, nkeys), 1)
    dist = jnp.abs(qi + BLOCK - kj)
    distf = dist.astype(F32)
    band = dist <= WINDOW
    low_lanes = lax.broadcasted_iota(jnp.int32, (BLOCK, LANES), 1) < SWA_HEAD_DIM
    top_rows = lax.broadcasted_iota(jnp.int32, (2 * BLOCK, 1), 0) < BLOCK
    half = SWA_HEADS // 2
    grp_heads = SWA_HEADS // SWA_KV_HEADS
    assert half == grp_heads
    slopes = [2.0 ** (-8.0 * (h + 1) / SWA_HEADS) for h in range(SWA_HEADS)]
    zero = jnp.zeros((), BF16)
    for c in range(tq // BLOCK):
        kpos = i * tq + (c - 1) * BLOCK + kj
        valid = band & (kpos >= 0) & (kpos < seq_len)
        valid2 = jnp.concatenate([valid, valid], axis=0)
        kk = kv[c * BLOCK:c * BLOCK + nkeys, :SWA_KV]
        vv = kv[c * BLOCK:c * BLOCK + nkeys, SWA_KV:]
        outs = []
        for p in range(half):
            qp = main[c * BLOCK:(c + 1) * BLOCK, p * LANES:(p + 1) * LANES]
            q2 = jnp.concatenate([jnp.where(low_lanes, qp, zero), jnp.where(low_lanes, zero, qp)], axis=0)
            sc = lax.dot_general(q2, kk, NT_DIMS, preferred_element_type=F32) * (SWA_HEAD_DIM ** -0.5)
            bias = jnp.concatenate([distf * slopes[p], distf * slopes[p + half]], axis=0)
            sc = jnp.where(valid2, sc - bias, -jnp.inf)
            sink = jnp.where(top_rows, sink_ref[p], sink_ref[p + half])
            m = jnp.maximum(jnp.max(sc, axis=-1, keepdims=True), sink)
            pe = jnp.exp(sc - m)
            den = jnp.sum(pe, axis=-1, keepdims=True) + jnp.exp(sink - m)
            pv = jnp.dot(pe.astype(BF16), vv, preferred_element_type=F32) * (1.0 / den)
            outs.append(jnp.where(low_lanes, pv[:BLOCK], pv[BLOCK:]).astype(BF16))
        o_ref[0, c * BLOCK:(c + 1) * BLOCK, :] = jnp.concatenate(outs, axis=-1)


def _swa(proj, sink):
    b, s, _ = proj.shape
    tq = _tile(s, 512)
    per = tq // BLOCK
    nblk = s // BLOCK
    kv_col = (PROJ_SWA + SWA_Q) // (2 * SWA_KV)
    return pl.pallas_call(
        functools.partial(_swa_kernel, seq_len=s, tq=tq),
        grid=(b, s // tq),
        in_specs=[
            pl.BlockSpec(memory_space=pltpu.SMEM),
            pl.BlockSpec((1, tq, SWA_SLAB), lambda bi, i: (bi, i, PROJ_SWA // SWA_SLAB)),
            pl.BlockSpec((1, BLOCK, 2 * SWA_KV), lambda bi, i: (bi, jnp.maximum(i * per - 1, 0), kv_col)),
            pl.BlockSpec((1, BLOCK, 2 * SWA_KV), lambda bi, i: (bi, jnp.minimum((i + 1) * per, nblk - 1), kv_col)),
        ],
        out_specs=pl.BlockSpec((1, tq, SWA_Q), lambda bi, i: (bi, i, 0)),
        out_shape=jax.ShapeDtypeStruct((b, s, SWA_Q), BF16),
        compiler_params=_cparams(("parallel", "parallel"), 32),
        name="swa",
    )(sink, proj, proj, proj)


def _prefix_sum_rows(x):
    n = x.shape[0]
    row = lax.broadcasted_iota(jnp.int32, x.shape, 0)
    k = 1
    while k < n:
        x = x + jnp.where(row >= k, pltpu.roll(x, k, 0), 0.0)
        k *= 2
    return x


def _suffix_sum_rows(x):
    n = x.shape[0]
    row = lax.broadcasted_iota(jnp.int32, x.shape, 0)
    k = 1
    while k < n:
        x = x + jnp.where(row < n - k, pltpu.roll(x, n - k, 0), 0.0)
        k *= 2
    return x


def _ssd_decay_terms(dt_ref, alog_ref, dtb_ref):
    dt = _softplus(dt_ref[0] + dtb_ref[...])
    a = dt * (-jnp.exp(alog_ref[...]))
    pre = _prefix_sum_rows(a)
    suf = _suffix_sum_rows(a)
    fwd_cols = lax.broadcasted_iota(jnp.int32, a.shape, 1) < SSD_HEADS
    acum = jnp.where(fwd_cols, pre, suf)
    tot = pre[SSD_CHUNK - 1:SSD_CHUNK, :]
    return dt, acum, tot


def _pair_expand(t, p, d, low_lanes):
    c = d * SSD_HEADS + 2 * p
    return jnp.where(low_lanes, t[:, c:c + 1], t[:, c + 1:c + 2])


def _ssd_local_kernel(xp_ref, xm_ref, xn_ref, dt_ref, cw_ref, cb_ref, alog_ref, dtb_ref,
                      xc_ref, sf_ref, sb_ref, dec_ref, ext_scr):
    c = pl.program_id(1)
    nc = pl.num_programs(1)
    halo = BF16_ROWS
    r = SSD_CONV // 2
    ext_scr[0:halo] = jnp.where(c > 0, xp_ref[0].astype(F32), 0.0)
    ext_scr[halo:halo + SSD_CHUNK] = xm_ref[0].astype(F32)
    ext_scr[halo + SSD_CHUNK:] = jnp.where(c < nc - 1, xn_ref[0].astype(F32), 0.0)
    acc = jnp.broadcast_to(cb_ref[...], (SSD_CHUNK, SSD_XBC))
    for i in range(SSD_CONV):
        acc = acc + ext_scr[pl.ds(halo - r + i, SSD_CHUNK), :] * cw_ref[i:i + 1, :]
    xc = _silu(acc)
    xc_ref[0] = xc.astype(BF16)

    dt, acum, tot = _ssd_decay_terms(dt_ref, alog_ref, dtb_ref)
    wt = dt * jnp.exp(tot - acum)
    low_lanes = lax.broadcasted_iota(jnp.int32, (SSD_CHUNK, LANES), 1) < SSD_HEAD_DIM
    bts = [xc[:, SSD_INNER + g * SSD_STATE:SSD_INNER + (g + 1) * SSD_STATE].T.astype(BF16)
           for g in range(SSD_GROUPS)]
    for d, s_ref in enumerate((sf_ref, sb_ref)):
        xw = jnp.concatenate(
            [(xc[:, p * LANES:(p + 1) * LANES] * _pair_expand(wt, p, d, low_lanes)).astype(BF16)
             for p in range(SSD_PAIRS)], axis=-1)
        for g in range(SSD_GROUPS):
            cols = slice(g * SSD_GROUP_W, (g + 1) * SSD_GROUP_W)
            s_ref[0, 0, :, cols] = jnp.dot(bts[g], xw[:, cols], preferred_element_type=F32)
    dec = jnp.broadcast_to(jnp.exp(tot), (8, LANES))
    low8 = low_lanes[:8]
    for d in range(2):
        for p in range(SSD_PAIRS):
            col = d * SSD_INNER + p * LANES
            dec_ref[0, 0, :, col:col + LANES] = _pair_expand(dec, p, d, low8)


def _ssd_local(proj, dt, cw, cb, alog, dtb):
    b, s, _ = proj.shape
    nc = s // SSD_CHUNK
    per = SSD_CHUNK // BF16_ROWS
    nhalo = s // BF16_ROWS
    const = lambda bi, c: (0, 0)
    return pl.pallas_call(
        _ssd_local_kernel,
        grid=(b, nc),
        in_specs=[
            pl.BlockSpec((1, BF16_ROWS, SSD_XBC), lambda bi, c: (bi, jnp.maximum(c * per - 1, 0), 0)),
            pl.BlockSpec((1, SSD_CHUNK, SSD_XBC), lambda bi, c: (bi, c, 0)),
            pl.BlockSpec((1, BF16_ROWS, SSD_XBC), lambda bi, c: (bi, jnp.minimum((c + 1) * per, nhalo - 1), 0)),
            pl.BlockSpec((1, SSD_CHUNK, DT_W), lambda bi, c: (bi, c, 0)),
            pl.BlockSpec((SSD_CONV, SSD_XBC), const),
            pl.BlockSpec((1, SSD_XBC), const),
            pl.BlockSpec((1, DT_W), const),
            pl.BlockSpec((1, DT_W), const),
        ],
        out_specs=[
            pl.BlockSpec((1, SSD_CHUNK, SSD_XBC), lambda bi, c: (bi, c, 0)),
            pl.BlockSpec((1, 1, SSD_STATE, SSD_INNER), lambda bi, c: (bi, c, 0, 0)),
            pl.BlockSpec((1, 1, SSD_STATE, SSD_INNER), lambda bi, c: (bi, c, 0, 0)),
            pl.BlockSpec((1, 1, 8, 2 * SSD_INNER), lambda bi, c: (bi, c, 0, 0)),
        ],
        out_shape=[
            jax.ShapeDtypeStruct((b, s, SSD_XBC), BF16),
            jax.ShapeDtypeStruct((b, nc, SSD_STATE, SSD_INNER), F32),
            jax.ShapeDtypeStruct((b, nc, SSD_STATE, SSD_INNER), F32),
            jax.ShapeDtypeStruct((b, nc, 8, 2 * SSD_INNER), F32),
        ],
        scratch_shapes=[pltpu.VMEM((SSD_CHUNK + 2 * BF16_ROWS, SSD_XBC), F32)],
        compiler_params=_cparams(("parallel", "parallel"), 32),
        name="ssd_local",
    )(proj, proj, proj, dt, cw, cb, alog, dtb)


def _ssd_scan_kernel(sf_ref, sb_ref, decf_ref, decb_ref, pf_ref, pb_ref, cf_scr, cb_scr):
    @pl.when(pl.program_id(1) == 0)
    def _():
        cf_scr[...] = jnp.zeros(cf_scr.shape, F32)
        cb_scr[...] = jnp.zeros(cb_scr.shape, F32)

    pf_ref[0, 0] = cf_scr[...].astype(BF16)
    pb_ref[0, 0] = cb_scr[...].astype(BF16)
    cf_scr[...] = cf_scr[...] * decf_ref[0, 0, 0:1, :] + sf_ref[0, 0]
    cb_scr[...] = cb_scr[...] * decb_ref[0, 0, 0:1, :] + sb_ref[0, 0]


def _ssd_scan(sf, sb, dec):
    b, nc, n, w = sf.shape
    fwd = lambda bi, c: (bi, c, 0, 0)
    bwd = lambda bi, c: (bi, nc - 1 - c, 0, 0)
    return pl.pallas_call(
        _ssd_scan_kernel,
        grid=(b, nc),
        in_specs=[
            pl.BlockSpec((1, 1, n, w), fwd),
            pl.BlockSpec((1, 1, n, w), bwd),
            pl.BlockSpec((1, 1, 8, w), fwd),
            pl.BlockSpec((1, 1, 8, w), lambda bi, c: (bi, nc - 1 - c, 0, 1)),
        ],
        out_specs=[pl.BlockSpec((1, 1, n, w), fwd), pl.BlockSpec((1, 1, n, w), bwd)],
        out_shape=[jax.ShapeDtypeStruct((b, nc, n, w), BF16), jax.ShapeDtypeStruct((b, nc, n, w), BF16)],
        scratch_shapes=[pltpu.VMEM((n, w), F32), pltpu.VMEM((n, w), F32)],
        compiler_params=_cparams(("parallel", "arbitrary"), 32),
        name="ssd_scan",
    )(sf, sb, dec, dec)


def _ssd_out_kernel(xc_ref, dt_ref, z_ref, pf_ref, pb_ref, alog_ref, dtb_ref, dskip_ref, nw_ref, o_ref):
    dt, acum, _ = _ssd_decay_terms(dt_ref, alog_ref, dtb_ref)
    acum_t = acum.T
    dt_t = dt.T
    eac = jnp.exp(acum)
    row = lax.broadcasted_iota(jnp.int32, (SSD_CHUNK, SSD_CHUNK), 0)
    col = lax.broadcasted_iota(jnp.int32, (SSD_CHUNK, SSD_CHUNK), 1)
    masks = (row >= col, row <= col)
    low_lanes = lax.broadcasted_iota(jnp.int32, (SSD_CHUNK, LANES), 1) < SSD_HEAD_DIM
    zero = jnp.zeros((), BF16)
    pairs_per_group = SSD_PAIRS // SSD_GROUPS
    for g in range(SSD_GROUPS):
        gcols = slice(g * SSD_GROUP_W, (g + 1) * SSD_GROUP_W)
        bm = xc_ref[0, :, SSD_INNER + g * SSD_STATE:SSD_INNER + (g + 1) * SSD_STATE]
        cm_lo = SSD_INNER + SSD_GROUPS * SSD_STATE + g * SSD_STATE
        cm = xc_ref[0, :, cm_lo:cm_lo + SSD_STATE]
        cb = lax.dot_general(cm, bm, NT_DIMS, preferred_element_type=F32)
        y_off = (jnp.dot(cm, pf_ref[0, 0, :, gcols], preferred_element_type=F32),
                 jnp.dot(cm, pb_ref[0, 0, :, gcols], preferred_element_type=F32))
        ys = []
        for pp in range(pairs_per_group):
            p = g * pairs_per_group + pp
            mats = []
            for h in (2 * p, 2 * p + 1):
                w = None
                for d in range(2):
                    k = d * SSD_HEADS + h
                    seg = acum[:, k:k + 1] - acum_t[k:k + 1, :]
                    e = jnp.exp(jnp.where(masks[d], seg, -jnp.inf)) * dt_t[k:k + 1, :]
                    w = e if w is None else w + e
                mats.append((cb * w).astype(BF16))
            lhs = jnp.concatenate(mats, axis=-1)
            xs = xc_ref[0, :, p * LANES:(p + 1) * LANES]
            rhs = jnp.concatenate([jnp.where(low_lanes, xs, zero), jnp.where(low_lanes, zero, xs)], axis=0)
            y = jnp.dot(lhs, rhs, preferred_element_type=F32)
            lanes = slice(pp * LANES, (pp + 1) * LANES)
            for d in range(2):
                y = y + y_off[d][:, lanes] * _pair_expand(eac, p, d, low_lanes)
            y = y + xs.astype(F32) * dskip_ref[:, p * LANES:(p + 1) * LANES]
            y = y * _silu(z_ref[0, :, p * LANES:(p + 1) * LANES].astype(F32))
            ys.append(y)
        yg = jnp.concatenate(ys, axis=-1)
        yg = yg * lax.rsqrt(jnp.mean(yg * yg, axis=-1, keepdims=True) + EPS)
        o_ref[0, :, gcols] = (yg * nw_ref[:, gcols]).astype(BF16)


def _ssd_out(xc, dt, proj, pf, pb, alog, dtb, dskip, nw):
    b, s, _ = xc.shape
    nc = s // SSD_CHUNK
    const = lambda bi, c: (0, 0)
    chunk = lambda bi, c: (bi, c, 0)
    state = lambda bi, c: (bi, c, 0, 0)
    return pl.pallas_call(
        _ssd_out_kernel,
        grid=(b, nc),
        in_specs=[
            pl.BlockSpec((1, SSD_CHUNK, SSD_XBC), chunk),
            pl.BlockSpec((1, SSD_CHUNK, DT_W), chunk),
            pl.BlockSpec((1, SSD_CHUNK, SSD_INNER), lambda bi, c: (bi, c, PROJ_Z // SSD_INNER)),
            pl.BlockSpec((1, 1, SSD_STATE, SSD_INNER), state),
            pl.BlockSpec((1, 1, SSD_STATE, SSD_INNER), state),
            pl.BlockSpec((1, DT_W), const),
            pl.BlockSpec((1, DT_W), const),
            pl.BlockSpec((1, SSD_INNER), const),
            pl.BlockSpec((1, SSD_INNER), const),
        ],
        out_specs=pl.BlockSpec((1, SSD_CHUNK, SSD_INNER), chunk),
        out_shape=jax.ShapeDtypeStruct((b, s, SSD_INNER), BF16),
        compiler_params=_cparams(("parallel", "parallel"), 32),
        name="ssd_out",
    )(xc, dt, proj, pf, pb, alog, dtb, dskip, nw)


def _out_proj_kernel(oa_ref, ob_ref, oc_ref, wa_ref, wb_ref, wc_ref, x_ref, g_ref, o_ref):
    y = jnp.dot(oa_ref[0], wa_ref[...], preferred_element_type=F32)
    y = y + jnp.dot(ob_ref[0], wb_ref[...], preferred_element_type=F32)
    y = y + jnp.dot(oc_ref[0], wc_ref[...], preferred_element_type=F32)
    o_ref[0] = x_ref[0] + _rms(y, g_ref[...])


def _out_proj(oa, ob, oc, wa, wb, wc, x, g):
    b, s, d = x.shape
    tm = _tile(s, 512)
    const = lambda bi, i: (0, 0)
    row = lambda bi, i: (bi, i, 0)
    return pl.pallas_call(
        _out_proj_kernel,
        grid=(b, s // tm),
        in_specs=[
            pl.BlockSpec((1, tm, oa.shape[-1]), row),
            pl.BlockSpec((1, tm, ob.shape[-1]), row),
            pl.BlockSpec((1, tm, oc.shape[-1]), row),
            pl.BlockSpec(wa.shape, const),
            pl.BlockSpec(wb.shape, const),
            pl.BlockSpec(wc.shape, const),
            pl.BlockSpec((1, tm, d), row),
            pl.BlockSpec((1, d), const),
        ],
        out_specs=pl.BlockSpec((1, tm, d), row),
        out_shape=jax.ShapeDtypeStruct((b, s, d), F32),
        compiler_params=_cparams(("parallel", "parallel"), 48),
        name="out_proj",
    )(oa, ob, oc, wa, wb, wc, x, g)


def _mem_kv_kernel(m_ref, g_ref, w_ref, o_ref):
    h = _rms(m_ref[0], g_ref[...]).astype(BF16)
    o_ref[0] = jnp.dot(h, w_ref[...], preferred_element_type=F32).astype(BF16)


def _mem_kv(mem, g, w):
    b, n, d = mem.shape
    const = lambda bi: (0, 0)
    return pl.pallas_call(
        _mem_kv_kernel,
        grid=(b,),
        in_specs=[
            pl.BlockSpec((1, n, d), lambda bi: (bi, 0, 0)),
            pl.BlockSpec((1, d), const),
            pl.BlockSpec(w.shape, const),
        ],
        out_specs=pl.BlockSpec((1, n, w.shape[1]), lambda bi: (bi, 0, 0)),
        out_shape=jax.ShapeDtypeStruct((b, n, w.shape[1]), BF16),
        compiler_params=_cparams(("parallel",), 32),
        name="mem_kv",
    )(mem, g, w)


def _xattn_kernel(x_ref, gpre_ref, wq_ref, kv_ref, wo_ref, gpost_ref, o_ref):
    x = x_ref[0]
    h = _rms(x, gpre_ref[...]).astype(BF16)
    q = (jnp.dot(h, wq_ref[...], preferred_element_type=F32) * (X_HEAD_DIM ** -0.5)).astype(BF16)
    outs = []
    for hd in range(X_HEADS):
        lanes = slice(hd * X_HEAD_DIM, (hd + 1) * X_HEAD_DIM)
        kh = kv_ref[0, :, lanes]
        vh = kv_ref[0, :, X_WIDTH + hd * X_HEAD_DIM:X_WIDTH + (hd + 1) * X_HEAD_DIM]
        s = lax.dot_general(q[:, lanes], kh, NT_DIMS, preferred_element_type=F32)
        p = jnp.exp(s - jnp.max(s, axis=-1, keepdims=True))
        inv = 1.0 / jnp.sum(p, axis=-1, keepdims=True)
        outs.append((jnp.dot(p.astype(BF16), vh, preferred_element_type=F32) * inv).astype(BF16))
    o = jnp.concatenate(outs, axis=-1)
    y = jnp.dot(o, wo_ref[...], preferred_element_type=F32)
    o_ref[0] = x + _rms(y, gpost_ref[...])


def _xattn(x, gpre, wq, kv, wo, gpost):
    b, s, d = x.shape
    tm = _tile(s, 512)
    const = lambda bi, i: (0, 0)
    row = lambda bi, i: (bi, i, 0)
    return pl.pallas_call(
        _xattn_kernel,
        grid=(b, s // tm),
        in_specs=[
            pl.BlockSpec((1, tm, d), row),
            pl.BlockSpec((1, d), const),
            pl.BlockSpec(wq.shape, const),
            pl.BlockSpec((1,) + kv.shape[1:], lambda bi, i: (bi, 0, 0)),
            pl.BlockSpec(wo.shape, const),
            pl.BlockSpec((1, d), const),
        ],
        out_specs=pl.BlockSpec((1, tm, d), row),
        out_shape=jax.ShapeDtypeStruct((b, s, d), F32),
        compiler_params=_cparams(("parallel", "parallel"), 48),
        name="xattn",
    )(x, gpre, wq, kv, wo, gpost)


def _ffn_up_kernel(xp_ref, xm_ref, xn_ref, g_ref, wg_ref, wu_ref, cw_ref, cb_ref, o_ref, h_scr, g_scr, *, tm):
    i = pl.program_id(1)
    halo = BF16_ROWS

    @pl.when(pl.program_id(2) == 0)
    def _():
        g = g_ref[...]
        first = i == 0
        last = i == pl.num_programs(1) - 1
        h_scr[0:halo] = jnp.where(first, 0.0, _rms(xp_ref[0], g)).astype(BF16)
        h_scr[halo:halo + tm] = _rms(xm_ref[0], g).astype(BF16)
        h_scr[halo + tm:] = jnp.where(last, 0.0, _rms(xn_ref[0], g)).astype(BF16)

    g_scr[...] = jnp.dot(h_scr[...], wg_ref[...], preferred_element_type=F32)
    u = jnp.dot(h_scr[halo:halo + tm], wu_ref[...], preferred_element_type=F32)
    r = FFN_CONV // 2
    conv = jnp.broadcast_to(cb_ref[...], u.shape)
    for k in range(FFN_CONV):
        conv = conv + g_scr[pl.ds(halo - r + k, tm), :] * cw_ref[k:k + 1, :]
    o_ref[0] = (_silu(conv) * u).astype(BF16)


def _ffn_up(x, g, w_up, cw, cb):
    b, s, d = x.shape
    tm = _tile(s, 1024)
    tn = 512
    nj = D_FF // tn
    per = tm // BF16_ROWS
    nhalo = s // BF16_ROWS
    const = lambda bi, i, j: (0, 0)
    return pl.pallas_call(
        functools.partial(_ffn_up_kernel, tm=tm),
        grid=(b, s // tm, nj),
        in_specs=[
            pl.BlockSpec((1, BF16_ROWS, d), lambda bi, i, j: (bi, jnp.maximum(i * per - 1, 0), 0)),
            pl.BlockSpec((1, tm, d), lambda bi, i, j: (bi, i, 0)),
            pl.BlockSpec((1, BF16_ROWS, d), lambda bi, i, j: (bi, jnp.minimum((i + 1) * per, nhalo - 1), 0)),
            pl.BlockSpec((1, d), const),
            pl.BlockSpec((d, tn), lambda bi, i, j: (0, j)),
            pl.BlockSpec((d, tn), lambda bi, i, j: (0, j + nj)),
            pl.BlockSpec((FFN_CONV, tn), lambda bi, i, j: (0, j)),
            pl.BlockSpec((1, tn), lambda bi, i, j: (0, j)),
        ],
        out_specs=pl.BlockSpec((1, tm, tn), lambda bi, i, j: (bi, i, j)),
        out_shape=jax.ShapeDtypeStruct((b, s, D_FF), BF16),
        scratch_shapes=[
            pltpu.VMEM((tm + 2 * BF16_ROWS, d), BF16),
            pltpu.VMEM((tm + 2 * BF16_ROWS, tn), F32),
        ],
        compiler_params=_cparams(("parallel", "parallel", "arbitrary"), 48),
        name="ffn_up",
    )(x, x, x, g, w_up, w_up, cw, cb)


def _ffn_down_kernel(a_ref, w_ref, x_ref, g_ref, o_ref, acc_scr):
    k = pl.program_id(2)

    @pl.when(k == 0)
    def _():
        acc_scr[...] = jnp.zeros(acc_scr.shape, F32)

    acc_scr[...] += jnp.dot(a_ref[0], w_ref[...], preferred_element_type=F32)

    @pl.when(k == pl.num_programs(2) - 1)
    def _():
        o_ref[0] = x_ref[0] + _rms(acc_scr[...], g_ref[...])


def _ffn_down(a, w, x, g):
    b, s, d = x.shape
    tm = _tile(s, 1024)
    tk = 512
    return pl.pallas_call(
        _ffn_down_kernel,
        grid=(b, s // tm, D_FF // tk),
        in_specs=[
            pl.BlockSpec((1, tm, tk), lambda bi, i, k: (bi, i, k)),
            pl.BlockSpec((tk, d), lambda bi, i, k: (k, 0)),
            pl.BlockSpec((1, tm, d), lambda bi, i, k: (bi, i, 0)),
            pl.BlockSpec((1, d), lambda bi, i, k: (0, 0)),
        ],
        out_specs=pl.BlockSpec((1, tm, d), lambda bi, i, k: (bi, i, 0)),
        out_shape=jax.ShapeDtypeStruct((b, s, d), F32),
        scratch_shapes=[pltpu.VMEM((tm, d), F32)],
        compiler_params=_cparams(("parallel", "parallel", "arbitrary"), 56),
        name="ffn_down",
    )(a, w, x, g)


def _rot_half_cols(w):
    half = w.shape[-1] // 2
    return jnp.concatenate([-w[..., half:], w[..., :half]], axis=-1)


def _swa_head_perm():
    half = SWA_HEADS // 2
    order = []
    for p in range(half):
        order += [p, p + half]
    idx = np.concatenate([np.arange(h * SWA_HEAD_DIM, (h + 1) * SWA_HEAD_DIM) for h in order])
    return idx


def _pad_lanes(v, width):
    return jnp.pad(v, [(0, 0)] * (v.ndim - 1) + [(0, width - v.shape[-1])])


def _prepare_params(p):
    cuts = np.cumsum([0, MLA_Q_RANK, MLA_KV_RANK, MLA_ROPE, SWA_Q, SWA_KV, SWA_KV, SSD_INNER, SSD_XBC, 2 * SSD_HEADS])
    w_in = p['w_in']
    part = lambda k: w_in[:, :, cuts[k]:cuts[k + 1]]
    cq, ckv, kr, q_s, k_s, v_s, z, xbc, dtw = (part(k) for k in range(9))
    perm = _swa_head_perm()
    w_main = jnp.concatenate([xbc, cq, ckv, kr, _rot_half_cols(kr), q_s[:, :, perm], k_s, v_s, z], axis=-1)
    assert w_main.shape[-1] == PROJ_W

    uq = p['mla_w_uq'].reshape(DEPTH, MLA_Q_RANK, MLA_HEADS, MLA_NOPE + MLA_ROPE)
    uq_rope = uq[..., MLA_NOPE:]
    uq = jnp.concatenate([uq[..., :MLA_NOPE], uq_rope, _rot_half_cols(uq_rope)], axis=-1)
    ukv = p['mla_w_ukv'].reshape(DEPTH, MLA_KV_RANK, MLA_HEADS, MLA_NOPE + MLA_V)
    ukv = jnp.concatenate([ukv[..., :MLA_NOPE].reshape(DEPTH, MLA_KV_RANK, -1),
                           ukv[..., MLA_NOPE:].reshape(DEPTH, MLA_KV_RANK, -1)], axis=-1)

    w_out = p['w_out']
    a_w = MLA_HEADS * MLA_V
    row = lambda v: v[:, None, :].astype(F32)
    return dict(
        norm_mix_pre=row(p['norm_mix_pre']),
        w_main=w_main.astype(BF16),
        w_dt=_pad_lanes(dtw, DT_W).astype(BF16),
        mla_q_norm=row(p['mla_q_norm']),
        mla_kv_norm=row(p['mla_kv_norm']),
        mla_w_uq=uq.reshape(DEPTH, MLA_Q_RANK, MLA_HEADS * MLA_QK).astype(BF16),
        mla_w_ukv=ukv.astype(BF16),
        swa_sink=p['swa_sink'].astype(F32),
        ssd_conv_w=p['ssd_conv_w'].astype(F32),
        ssd_conv_b=row(p['ssd_conv_b']),
        ssd_a_log=_pad_lanes(p['ssd_a_log'].reshape(DEPTH, 1, 2 * SSD_HEADS).astype(F32), DT_W),
        ssd_dt_bias=_pad_lanes(p['ssd_dt_bias'].reshape(DEPTH, 1, 2 * SSD_HEADS).astype(F32), DT_W),
        ssd_d=jnp.repeat(p['ssd_d'].astype(F32), SSD_HEAD_DIM, axis=-1)[:, None, :],
        ssd_norm=row(p['ssd_norm']),
        w_out_a=w_out[:, :a_w].astype(BF16),
        w_out_b=w_out[:, a_w:a_w + SWA_Q][:, perm].astype(BF16),
        w_out_c=w_out[:, a_w + SWA_Q:].astype(BF16),
        norm_mix_post=row(p['norm_mix_post']),
        norm_mem_pre=row(p['norm_mem_pre']),
        norm_mem_post=row(p['norm_mem_post']),
        norm_mem_kv=row(p['norm_mem_kv']),
        xattn_wq=p['xattn_wq'].astype(BF16),
        xattn_wkv=p['xattn_wkv'].astype(BF16),
        xattn_wo=p['xattn_wo'].astype(BF16),
        norm_ffn_pre=row(p['norm_ffn_pre']),
        norm_ffn_post=row(p['norm_ffn_post']),
        ffn_w_up=p['ffn_w_up'].astype(BF16),
        ffn_conv_w=p['ffn_conv_w'].astype(F32),
        ffn_conv_b=row(p['ffn_conv_b']),
        ffn_w_down=p['ffn_w_down'].astype(BF16),
    )


def _rope_table(s):
    half = MLA_ROPE // 2
    inv = ROPE_BASE ** (-jnp.arange(half, dtype=F32) / half)
    ang = jnp.arange(s).astype(F32)[:, None] * inv[None, :]
    cos, sin = jnp.cos(ang), jnp.sin(ang)
    return jnp.concatenate([cos, cos, sin, sin], axis=-1)


def _layer(x, mem, cs, w):
    proj, dt = _in_proj(x, w['norm_mix_pre'], w['w_main'], w['w_dt'])
    q, k, v = _mla_prep(proj, cs, w['mla_q_norm'], w['mla_kv_norm'], w['mla_w_uq'], w['mla_w_ukv'])
    o_a = _mla_flash(q, k, v)
    o_b = _swa(proj, w['swa_sink'])
    xc, sf, sb, dec = _ssd_local(proj, dt, w['ssd_conv_w'], w['ssd_conv_b'], w['ssd_a_log'], w['ssd_dt_bias'])
    pf, pb = _ssd_scan(sf, sb, dec)
    o_c = _ssd_out(xc, dt, proj, pf, pb, w['ssd_a_log'], w['ssd_dt_bias'], w['ssd_d'], w['ssd_norm'])
    x = _out_proj(o_a, o_b, o_c, w['w_out_a'], w['w_out_b'], w['w_out_c'], x, w['norm_mix_post'])
    kv = _mem_kv(mem, w['norm_mem_kv'], w['xattn_wkv'])
    x = _xattn(x, w['norm_mem_pre'], w['xattn_wq'], kv, w['xattn_wo'], w['norm_mem_post'])
    a = _ffn_up(x, w['norm_ffn_pre'], w['ffn_w_up'], w['ffn_conv_w'], w['ffn_conv_b'])
    return _ffn_down(a, w['ffn_w_down'], x, w['norm_ffn_post'])


def _trunk(x, mem, weights):
    cs = _rope_table(x.shape[1])
    for layer in range(DEPTH):
        x = _layer(x, mem, cs, {name: arr[layer] for name, arr in weights.items()})
    return x


def kernel(x_prompt, x_sample, mem_prompt, mem_sample, norm_mix_pre, norm_mix_post, w_in, mla_q_norm, mla_kv_norm, mla_w_uq, mla_w_ukv, swa_sink, ssd_conv_w, ssd_conv_b, ssd_a_log, ssd_dt_bias, ssd_d, ssd_norm, w_out, norm_mem_pre, norm_mem_post, norm_mem_kv, xattn_wq, xattn_wkv, xattn_wo, norm_ffn_pre, norm_ffn_post, ffn_w_up, ffn_conv_w, ffn_conv_b, ffn_w_down):
    weights = _prepare_params(dict(
        norm_mix_pre=norm_mix_pre, norm_mix_post=norm_mix_post, w_in=w_in,
        mla_q_norm=mla_q_norm, mla_kv_norm=mla_kv_norm, mla_w_uq=mla_w_uq, mla_w_ukv=mla_w_ukv,
        swa_sink=swa_sink, ssd_conv_w=ssd_conv_w, ssd_conv_b=ssd_conv_b, ssd_a_log=ssd_a_log,
        ssd_dt_bias=ssd_dt_bias, ssd_d=ssd_d, ssd_norm=ssd_norm, w_out=w_out,
        norm_mem_pre=norm_mem_pre, norm_mem_post=norm_mem_post, norm_mem_kv=norm_mem_kv,
        xattn_wq=xattn_wq, xattn_wkv=xattn_wkv, xattn_wo=xattn_wo,
        norm_ffn_pre=norm_ffn_pre, norm_ffn_post=norm_ffn_post, ffn_w_up=ffn_w_up,
        ffn_conv_w=ffn_conv_w, ffn_conv_b=ffn_conv_b, ffn_w_down=ffn_w_down))
    return (_trunk(x_prompt, mem_prompt, weights), _trunk(x_sample, mem_sample, weights))
```

```python
import functools

import numpy as np
import jax
import jax.numpy as jnp
from jax import lax
from jax.experimental import pallas as pl
from jax.experimental.pallas import tpu as pltpu

F32 = jnp.float32
BF16 = jnp.bfloat16

D_MODEL = 2048
DEPTH = 4
N_MEM = 256
BLOCK = 128
EPS = 1e-6

MLA_HEADS = 4
MLA_Q_RANK = 384
MLA_KV_RANK = 256
MLA_NOPE = 128
MLA_ROPE = 64
MLA_V = 128
ROPE_BASE = 10000.0
MLA_QK = 2 * MLA_NOPE

SWA_HEADS = 8
SWA_KV_HEADS = 2
SWA_HEAD_DIM = 64
SWA_Q = SWA_HEADS * SWA_HEAD_DIM
SWA_KV = SWA_KV_HEADS * SWA_HEAD_DIM
WINDOW = 128

SSD_HEADS = 16
SSD_HEAD_DIM = 64
SSD_GROUPS = 2
SSD_STATE = 128
SSD_CONV = 5
SSD_CHUNK = 128
SSD_INNER = SSD_HEADS * SSD_HEAD_DIM
SSD_XBC = SSD_INNER + 2 * SSD_GROUPS * SSD_STATE
SSD_GROUP_W = SSD_INNER // SSD_GROUPS
SSD_PAIRS = SSD_HEADS // 2

X_HEADS = 4
X_HEAD_DIM = 128
X_WIDTH = X_HEADS * X_HEAD_DIM

D_FF = 5632
FFN_CONV = 3

LANES = 128
BF16_ROWS = 16
VMEM_MIB = 2 ** 20

PROJ_XBC = 0
PROJ_MLA = SSD_XBC
MLA_SLAB = MLA_Q_RANK + MLA_KV_RANK + 2 * MLA_ROPE
PROJ_SWA = PROJ_MLA + MLA_SLAB
SWA_SLAB = SWA_Q + 2 * SWA_KV
PROJ_Z = PROJ_SWA + SWA_SLAB
PROJ_W = PROJ_Z + SSD_INNER
DT_W = LANES

MLA_VT_ROWS = MLA_V + BF16_ROWS
MLA_TK = 512
MLA_TQ = 2048
MLA_SUB = 512

NT_DIMS = (((1,), (1,)), ((), ()))


def _cparams(semantics, vmem_mib):
    return pltpu.CompilerParams(dimension_semantics=semantics, vmem_limit_bytes=vmem_mib * VMEM_MIB)


def _rms(x, g):
    ms = jnp.mean(x * x, axis=-1, keepdims=True)
    return x * lax.rsqrt(ms + EPS) * g


def _silu(x):
    return x * jax.nn.sigmoid(x)


def _softplus(x):
    return jnp.maximum(x, 0.0) + jnp.log1p(jnp.exp(-jnp.abs(x)))


def _tile(n, pref):
    return pref if n % pref == 0 else n


def _wspec(arr, layer, rank):
    tail = arr.shape[1:]
    idx = (layer,) + (0,) * len(tail)
    if rank == 1:
        return pl.BlockSpec((None,) + tail, lambda a: idx)
    if rank == 2:
        return pl.BlockSpec((None,) + tail, lambda a, b: idx)
    return pl.BlockSpec((None,) + tail, lambda a, b, c: idx)


def _in_proj_kernel(x_ref, g_ref, w_ref, wdt_ref, o_ref, dt_ref, h_scr):
    @pl.when(pl.program_id(2) == 0)
    def _():
        h = _rms(x_ref[0], g_ref[...]).astype(BF16)
        h_scr[...] = h
        dt_ref[0] = jnp.dot(h, wdt_ref[...], preferred_element_type=F32)

    o_ref[0] = jnp.dot(h_scr[...], w_ref[...], preferred_element_type=F32).astype(BF16)


def _in_proj(x, g, w, wdt, layer):
    b, s, d = x.shape
    tm = _tile(s, 1024)
    tn = 1024
    return pl.pallas_call(
        _in_proj_kernel,
        grid=(b, s // tm, PROJ_W // tn),
        in_specs=[
            pl.BlockSpec((1, tm, d), lambda bi, i, j: (bi, i, 0)),
            _wspec(g, layer, 3),
            pl.BlockSpec((None, d, tn), lambda bi, i, j: (layer, 0, j)),
            _wspec(wdt, layer, 3),
        ],
        out_specs=[
            pl.BlockSpec((1, tm, tn), lambda bi, i, j: (bi, i, j)),
            pl.BlockSpec((1, tm, DT_W), lambda bi, i, j: (bi, i, 0)),
        ],
        out_shape=[
            jax.ShapeDtypeStruct((b, s, PROJ_W), BF16),
            jax.ShapeDtypeStruct((b, s, DT_W), F32),
        ],
        scratch_shapes=[pltpu.VMEM((tm, d), BF16)],
        compiler_params=_cparams(("parallel", "parallel", "arbitrary"), 48),
        name="in_proj",
    )(x, g, w, wdt)


def _mla_prep_kernel(a_ref, cs_ref, cst_ref, qn_ref, kvn_ref, wuqt_ref, wuk_ref, wuvt_ref, qt_ref, k_ref, vt_ref):
    a = a_ref[0].astype(F32)
    cq = a[:, :MLA_Q_RANK]
    ckv = a[:, MLA_Q_RANK:MLA_Q_RANK + MLA_KV_RANK]
    kr = a[:, MLA_Q_RANK + MLA_KV_RANK:]
    scale = (MLA_NOPE + MLA_ROPE) ** -0.5
    cqn = _rms(cq, qn_ref[...]).astype(BF16)
    ckvn = _rms(ckv, kvn_ref[...]).astype(BF16)
    qt = lax.dot_general(wuqt_ref[...], cqn, NT_DIMS, preferred_element_type=F32) * scale
    kn = jnp.dot(ckvn, wuk_ref[...], preferred_element_type=F32)
    vt = lax.dot_general(wuvt_ref[...], ckvn, NT_DIMS, preferred_element_type=F32)
    t = kr * cs_ref[...]
    k_rope = (t + pltpu.roll(t, MLA_ROPE, 1)).astype(BF16)
    cst = cst_ref[...]
    ones = jnp.ones((BF16_ROWS, vt.shape[1]), BF16)
    for h in range(MLA_HEADS):
        lo = h * MLA_QK
        mid = lo + MLA_NOPE
        hi = lo + MLA_QK
        qt_ref[0, lo:mid, :] = qt[lo:mid].astype(BF16)
        qt_ref[0, mid:hi, :] = (qt[mid:hi] * cst).astype(BF16)
        k_ref[0, :, lo:mid] = kn[:, h * MLA_NOPE:(h + 1) * MLA_NOPE].astype(BF16)
        k_ref[0, :, mid:hi] = k_rope
        vt_ref[0, h, 0, :MLA_V, :] = vt[h * MLA_V:(h + 1) * MLA_V].astype(BF16)
        vt_ref[0, h, 0, MLA_V:, :] = ones


def _mla_prep(proj, cs, cst, qn, kvn, wuqt, wuk, wuvt, layer):
    b, s, _ = proj.shape
    tm = _tile(s, MLA_TK)
    return pl.pallas_call(
        _mla_prep_kernel,
        grid=(b, s // tm),
        in_specs=[
            pl.BlockSpec((1, tm, MLA_SLAB), lambda bi, i: (bi, i, PROJ_MLA // MLA_SLAB)),
            pl.BlockSpec((tm, LANES), lambda bi, i: (i, 0)),
            pl.BlockSpec((LANES, tm), lambda bi, i: (0, i)),
            _wspec(qn, layer, 2),
            _wspec(kvn, layer, 2),
            _wspec(wuqt, layer, 2),
            _wspec(wuk, layer, 2),
            _wspec(wuvt, layer, 2),
        ],
        out_specs=[
            pl.BlockSpec((1, MLA_HEADS * MLA_QK, tm), lambda bi, i: (bi, 0, i)),
            pl.BlockSpec((1, tm, MLA_HEADS * MLA_QK), lambda bi, i: (bi, i, 0)),
            pl.BlockSpec((1, MLA_HEADS, 1, MLA_VT_ROWS, tm), lambda bi, i: (bi, 0, i, 0, 0)),
        ],
        out_shape=[
            jax.ShapeDtypeStruct((b, MLA_HEADS * MLA_QK, s), BF16),
            jax.ShapeDtypeStruct((b, s, MLA_HEADS * MLA_QK), BF16),
            jax.ShapeDtypeStruct((b, MLA_HEADS, s // tm, MLA_VT_ROWS, tm), BF16),
        ],
        compiler_params=_cparams(("parallel", "parallel"), 32),
        name="mla_prep",
    )(proj, cs, cst, qn, kvn, wuqt, wuk, wuvt)


def _mla_flash_kernel(qt_ref, k_ref, vt_ref, o_ref, *scr, nsub):
    tk = vt_ref.shape[4]
    nck = vt_ref.shape[2]
    sub = qt_ref.shape[2] // nsub
    m_scrs = scr[:nsub]
    acc_scrs = scr[nsub:2 * nsub]
    s_scrs = scr[2 * nsub:]

    def scores(kc, t):
        k = k_ref[0, pl.ds(pl.multiple_of(kc * tk, tk), tk), :]
        return jnp.dot(k, qt_ref[0, :, t * sub:(t + 1) * sub], preferred_element_type=F32)

    for t in range(nsub):
        m_scrs[t][...] = jnp.full(m_scrs[t].shape, -jnp.inf, F32)
        acc_scrs[t][...] = jnp.zeros(acc_scrs[t].shape, F32)
        s_scrs[t][...] = scores(0, t)

    def body(kc, carry):
        vt = vt_ref[0, 0, kc]
        nxt = jnp.minimum(kc + 1, nck - 1)
        for t in range(nsub):
            st = s_scrs[t][...]
            s_scrs[t][...] = scores(nxt, t)
            m_prev = m_scrs[t][...]
            m_new = jnp.maximum(m_prev, jnp.max(st, axis=0, keepdims=True))
            alpha = jnp.exp(m_prev - m_new)
            p = jnp.exp(st - m_new).astype(BF16)
            m_scrs[t][...] = m_new
            acc_scrs[t][...] = alpha * acc_scrs[t][...] + jnp.dot(vt, p, preferred_element_type=F32)
        return carry

    lax.fori_loop(0, nck, body, 0)
    acc = jnp.concatenate([a[...] for a in acc_scrs], axis=1)
    o = acc[:MLA_V] * (1.0 / acc[MLA_V:MLA_V + 1])
    o_ref[0] = o.T.astype(BF16)


def _mla_flash(qt, k, vt):
    b, s, _ = k.shape
    nck, rows, tk = vt.shape[2:]
    tq = _tile(s, MLA_TQ)
    sub = _tile(tq, MLA_SUB)
    nsub = tq // sub
    return pl.pallas_call(
        functools.partial(_mla_flash_kernel, nsub=nsub),
        grid=(b, MLA_HEADS, s // tq),
        in_specs=[
            pl.BlockSpec((1, MLA_QK, tq), lambda bi, h, i: (bi, h, i)),
            pl.BlockSpec((1, s, MLA_QK), lambda bi, h, i: (bi, 0, h)),
            pl.BlockSpec((1, 1, nck, rows, tk), lambda bi, h, i: (bi, h, 0, 0, 0)),
        ],
        out_specs=pl.BlockSpec((1, tq, MLA_V), lambda bi, h, i: (bi, i, h)),
        out_shape=jax.ShapeDtypeStruct((b, s, MLA_HEADS * MLA_V), BF16),
        scratch_shapes=([pltpu.VMEM((1, sub), F32)] * nsub + [pltpu.VMEM((rows, sub), F32)] * nsub
                        + [pltpu.VMEM((tk, sub), F32)] * nsub),
        compiler_params=_cparams(("parallel", "parallel", "arbitrary"), 48),
        name="mla_flash",
    )(qt, k, vt)


def _swa_kernel(sink_ref, main_ref, prev_ref, next_ref, o_ref, *, seq_len, tq, layer):
    i = pl.program_id(1)
    main = main_ref[0]
    kv = jnp.concatenate([prev_ref[0], main[:, SWA_Q:], next_ref[0]], axis=0)
    nkeys = 3 * BLOCK
    qi = lax.broadcasted_iota(jnp.int32, (BLOCK, nkeys), 0)
    kj = lax.broadcasted_iota(jnp.int32, (BLOCK, nkeys), 1)
    dist = jnp.abs(qi + BLOCK - kj)
    distf = dist.astype(F32)
    band = dist <= WINDOW
    low_lanes = lax.broadcasted_iota(jnp.int32, (BLOCK, LANES), 1) < SWA_HEAD_DIM
    top_rows = lax.broadcasted_iota(jnp.int32, (2 * BLOCK, 1), 0) < BLOCK
    half = SWA_HEADS // 2
    grp_heads = SWA_HEADS // SWA_KV_HEADS
    assert half == grp_heads
    slopes = [2.0 ** (-8.0 * (h + 1) / SWA_HEADS) for h in range(SWA_HEADS)]
    zero = jnp.zeros((), BF16)
    for c in range(tq // BLOCK):
        kpos = i * tq + (c - 1) * BLOCK + kj
        valid = band & (kpos >= 0) & (kpos < seq_len)
        valid2 = jnp.concatenate([valid, valid], axis=0)
        kk = kv[c * BLOCK:c * BLOCK + nkeys, :SWA_KV]
        vv = kv[c * BLOCK:c * BLOCK + nkeys, SWA_KV:]
        outs = []
        for p in range(half):
            qp = main[c * BLOCK:(c + 1) * BLOCK, p * LANES:(p + 1) * LANES]
            q2 = jnp.concatenate([jnp.where(low_lanes, qp, zero), jnp.where(low_lanes, zero, qp)], axis=0)
            sc = lax.dot_general(q2, kk, NT_DIMS, preferred_element_type=F32) * (SWA_HEAD_DIM ** -0.5)
            bias = jnp.concatenate([distf * slopes[p], distf * slopes[p + half]], axis=0)
            sc = jnp.where(valid2, sc - bias, -jnp.inf)
            sink = jnp.where(top_rows, sink_ref[layer, p], sink_ref[layer, p + half])
            m = jnp.maximum(jnp.max(sc, axis=-1, keepdims=True), sink)
            pe = jnp.exp(sc - m)
            den = jnp.sum(pe, axis=-1, keepdims=True) + jnp.exp(sink - m)
            pv = jnp.dot(pe.astype(BF16), vv, preferred_element_type=F32) * (1.0 / den)
            outs.append(jnp.where(low_lanes, pv[:BLOCK], pv[BLOCK:]).astype(BF16))
        o_ref[0, c * BLOCK:(c + 1) * BLOCK, :] = jnp.concatenate(outs, axis=-1)


def _swa(proj, sink, layer):
    b, s, _ = proj.shape
    tq = _tile(s, 512)
    per = tq // BLOCK
    nblk = s // BLOCK
    kv_col = (PROJ_SWA + SWA_Q) // (2 * SWA_KV)
    return pl.pallas_call(
        functools.partial(_swa_kernel, seq_len=s, tq=tq, layer=layer),
        grid=(b, s // tq),
        in_specs=[
            pl.BlockSpec(memory_space=pltpu.SMEM),
            pl.BlockSpec((1, tq, SWA_SLAB), lambda bi, i: (bi, i, PROJ_SWA // SWA_SLAB)),
            pl.BlockSpec((1, BLOCK, 2 * SWA_KV), lambda bi, i: (bi, jnp.maximum(i * per - 1, 0), kv_col)),
            pl.BlockSpec((1, BLOCK, 2 * SWA_KV), lambda bi, i: (bi, jnp.minimum((i + 1) * per, nblk - 1), kv_col)),
        ],
        out_specs=pl.BlockSpec((1, tq, SWA_Q), lambda bi, i: (bi, i, 0)),
        out_shape=jax.ShapeDtypeStruct((b, s, SWA_Q), BF16),
        compiler_params=_cparams(("parallel", "parallel"), 32),
        name="swa",
    )(sink, proj, proj, proj)


def _prefix_sum_rows(x):
    n = x.shape[0]
    row = lax.broadcasted_iota(jnp.int32, x.shape, 0)
    k = 1
    while k < n:
        x = x + jnp.where(row >= k, pltpu.roll(x, k, 0), 0.0)
        k *= 2
    return x


def _suffix_sum_rows(x):
    n = x.shape[0]
    row = lax.broadcasted_iota(jnp.int32, x.shape, 0)
    k = 1
    while k < n:
        x = x + jnp.where(row < n - k, pltpu.roll(x, n - k, 0), 0.0)
        k *= 2
    return x


def _ssd_decay_terms(dt_ref, alog_ref, dtb_ref):
    dt = _softplus(dt_ref[0] + dtb_ref[...])
    a = dt * (-jnp.exp(alog_ref[...]))
    pre = _prefix_sum_rows(a)
    suf = _suffix_sum_rows(a)
    fwd_cols = lax.broadcasted_iota(jnp.int32, a.shape, 1) < SSD_HEADS
    acum = jnp.where(fwd_cols, pre, suf)
    tot = pre[SSD_CHUNK - 1:SSD_CHUNK, :]
    return dt, acum, tot


def _pair_expand(t, p, d, low_lanes):
    c = d * SSD_HEADS + 2 * p
    return jnp.where(low_lanes, t[:, c:c + 1], t[:, c + 1:c + 2])


def _ssd_local_kernel(xp_ref, xm_ref, xn_ref, dt_ref, cw_ref, cb_ref, alog_ref, dtb_ref,
                      xc_ref, sf_ref, sb_ref, dec_ref, ext_scr):
    c = pl.program_id(1)
    nc = pl.num_programs(1)
    halo = BF16_ROWS
    r = SSD_CONV // 2
    ext_scr[0:halo] = jnp.where(c > 0, xp_ref[0].astype(F32), 0.0)
    ext_scr[halo:halo + SSD_CHUNK] = xm_ref[0].astype(F32)
    ext_scr[halo + SSD_CHUNK:] = jnp.where(c < nc - 1, xn_ref[0].astype(F32), 0.0)
    acc = jnp.broadcast_to(cb_ref[...], (SSD_CHUNK, SSD_XBC))
    for i in range(SSD_CONV):
        acc = acc + ext_scr[pl.ds(halo - r + i, SSD_CHUNK), :] * cw_ref[i:i + 1, :]
    xc = _silu(acc)
    xc_ref[0] = xc.astype(BF16)

    dt, acum, tot = _ssd_decay_terms(dt_ref, alog_ref, dtb_ref)
    wt = dt * jnp.exp(tot - acum)
    low_lanes = lax.broadcasted_iota(jnp.int32, (SSD_CHUNK, LANES), 1) < SSD_HEAD_DIM
    bts = [xc[:, SSD_INNER + g * SSD_STATE:SSD_INNER + (g + 1) * SSD_STATE].T.astype(BF16)
           for g in range(SSD_GROUPS)]
    for d, s_ref in enumerate((sf_ref, sb_ref)):
        xw = jnp.concatenate(
            [(xc[:, p * LANES:(p + 1) * LANES] * _pair_expand(wt, p, d, low_lanes)).astype(BF16)
             for p in range(SSD_PAIRS)], axis=-1)
        for g in range(SSD_GROUPS):
            cols = slice(g * SSD_GROUP_W, (g + 1) * SSD_GROUP_W)
            s_ref[0, 0, :, cols] = jnp.dot(bts[g], xw[:, cols], preferred_element_type=F32)
    dec = jnp.broadcast_to(jnp.exp(tot), (8, LANES))
    low8 = low_lanes[:8]
    for d in range(2):
        for p in range(SSD_PAIRS):
            col = d * SSD_INNER + p * LANES
            dec_ref[0, 0, :, col:col + LANES] = _pair_expand(dec, p, d, low8)


def _ssd_local(proj, dt, cw, cb, alog, dtb, layer):
    b, s, _ = proj.shape
    nc = s // SSD_CHUNK
    per = SSD_CHUNK // BF16_ROWS
    nhalo = s // BF16_ROWS
    return pl.pallas_call(
        _ssd_local_kernel,
        grid=(b, nc),
        in_specs=[
            pl.BlockSpec((1, BF16_ROWS, SSD_XBC), lambda bi, c: (bi, jnp.maximum(c * per - 1, 0), 0)),
            pl.BlockSpec((1, SSD_CHUNK, SSD_XBC), lambda bi, c: (bi, c, 0)),
            pl.BlockSpec((1, BF16_ROWS, SSD_XBC), lambda bi, c: (bi, jnp.minimum((c + 1) * per, nhalo - 1), 0)),
            pl.BlockSpec((1, SSD_CHUNK, DT_W), lambda bi, c: (bi, c, 0)),
            _wspec(cw, layer, 2),
            _wspec(cb, layer, 2),
            _wspec(alog, layer, 2),
            _wspec(dtb, layer, 2),
        ],
        out_specs=[
            pl.BlockSpec((1, SSD_CHUNK, SSD_XBC), lambda bi, c: (bi, c, 0)),
            pl.BlockSpec((1, 1, SSD_STATE, SSD_INNER), lambda bi, c: (bi, c, 0, 0)),
            pl.BlockSpec((1, 1, SSD_STATE, SSD_INNER), lambda bi, c: (bi, c, 0, 0)),
            pl.BlockSpec((1, 1, 8, 2 * SSD_INNER), lambda bi, c: (bi, c, 0, 0)),
        ],
        out_shape=[
            jax.ShapeDtypeStruct((b, s, SSD_XBC), BF16),
            jax.ShapeDtypeStruct((b, nc, SSD_STATE, SSD_INNER), F32),
            jax.ShapeDtypeStruct((b, nc, SSD_STATE, SSD_INNER), F32),
            jax.ShapeDtypeStruct((b, nc, 8, 2 * SSD_INNER), F32),
        ],
        scratch_shapes=[pltpu.VMEM((SSD_CHUNK + 2 * BF16_ROWS, SSD_XBC), F32)],
        compiler_params=_cparams(("parallel", "parallel"), 32),
        name="ssd_local",
    )(proj, proj, proj, dt, cw, cb, alog, dtb)


def _ssd_scan_kernel(sf_ref, sb_ref, decf_ref, decb_ref, pf_ref, pb_ref, cf_scr, cb_scr):
    @pl.when(pl.program_id(1) == 0)
    def _():
        cf_scr[...] = jnp.zeros(cf_scr.shape, F32)
        cb_scr[...] = jnp.zeros(cb_scr.shape, F32)

    pf_ref[0, 0] = cf_scr[...].astype(BF16)
    pb_ref[0, 0] = cb_scr[...].astype(BF16)
    cf_scr[...] = cf_scr[...] * decf_ref[0, 0, 0:1, :] + sf_ref[0, 0]
    cb_scr[...] = cb_scr[...] * decb_ref[0, 0, 0:1, :] + sb_ref[0, 0]


def _ssd_scan(sf, sb, dec):
    b, nc, n, w = sf.shape
    fwd = lambda bi, c: (bi, c, 0, 0)
    bwd = lambda bi, c: (bi, nc - 1 - c, 0, 0)
    return pl.pallas_call(
        _ssd_scan_kernel,
        grid=(b, nc),
        in_specs=[
            pl.BlockSpec((1, 1, n, w), fwd),
            pl.BlockSpec((1, 1, n, w), bwd),
            pl.BlockSpec((1, 1, 8, w), fwd),
            pl.BlockSpec((1, 1, 8, w), lambda bi, c: (bi, nc - 1 - c, 0, 1)),
        ],
        out_specs=[pl.BlockSpec((1, 1, n, w), fwd), pl.BlockSpec((1, 1, n, w), bwd)],
        out_shape=[jax.ShapeDtypeStruct((b, nc, n, w), BF16), jax.ShapeDtypeStruct((b, nc, n, w), BF16)],
        scratch_shapes=[pltpu.VMEM((n, w), F32), pltpu.VMEM((n, w), F32)],
        compiler_params=_cparams(("parallel", "arbitrary"), 32),
        name="ssd_scan",
    )(sf, sb, dec, dec)


def _ssd_out_kernel(xc_ref, dt_ref, z_ref, pf_ref, pb_ref, alog_ref, dtb_ref, dskip_ref, nw_ref, o_ref):
    dt, acum, _ = _ssd_decay_terms(dt_ref, alog_ref, dtb_ref)
    acum_t = acum.T
    dt_t = dt.T
    eac = jnp.exp(acum)
    row = lax.broadcasted_iota(jnp.int32, (SSD_CHUNK, SSD_CHUNK), 0)
    col = lax.broadcasted_iota(jnp.int32, (SSD_CHUNK, SSD_CHUNK), 1)
    masks = (row >= col, row <= col)
    low_lanes = lax.broadcasted_iota(jnp.int32, (SSD_CHUNK, LANES), 1) < SSD_HEAD_DIM
    zero = jnp.zeros((), BF16)
    pairs_per_group = SSD_PAIRS // SSD_GROUPS
    for g in range(SSD_GROUPS):
        gcols = slice(g * SSD_GROUP_W, (g + 1) * SSD_GROUP_W)
        bm = xc_ref[0, :, SSD_INNER + g * SSD_STATE:SSD_INNER + (g + 1) * SSD_STATE]
        cm_lo = SSD_INNER + SSD_GROUPS * SSD_STATE + g * SSD_STATE
        cm = xc_ref[0, :, cm_lo:cm_lo + SSD_STATE]
        cb = lax.dot_general(cm, bm, NT_DIMS, preferred_element_type=F32)
        y_off = (jnp.dot(cm, pf_ref[0, 0, :, gcols], preferred_element_type=F32),
                 jnp.dot(cm, pb_ref[0, 0, :, gcols], preferred_element_type=F32))
        ys = []
        for pp in range(pairs_per_group):
            p = g * pairs_per_group + pp
            mats = []
            for h in (2 * p, 2 * p + 1):
                w = None
                for d in range(2):
                    k = d * SSD_HEADS + h
                    seg = acum[:, k:k + 1] - acum_t[k:k + 1, :]
                    e = jnp.exp(jnp.where(masks[d], seg, -jnp.inf)) * dt_t[k:k + 1, :]
                    w = e if w is None else w + e
                mats.append((cb * w).astype(BF16))
            lhs = jnp.concatenate(mats, axis=-1)
            xs = xc_ref[0, :, p * LANES:(p + 1) * LANES]
            rhs = jnp.concatenate([jnp.where(low_lanes, xs, zero), jnp.where(low_lanes, zero, xs)], axis=0)
            y = jnp.dot(lhs, rhs, preferred_element_type=F32)
            lanes = slice(pp * LANES, (pp + 1) * LANES)
            for d in range(2):
                y = y + y_off[d][:, lanes] * _pair_expand(eac, p, d, low_lanes)
            y = y + xs.astype(F32) * dskip_ref[:, p * LANES:(p + 1) * LANES]
            y = y * _silu(z_ref[0, :, p * LANES:(p + 1) * LANES].astype(F32))
            ys.append(y)
        yg = jnp.concatenate(ys, axis=-1)
        yg = yg * lax.rsqrt(jnp.mean(yg * yg, axis=-1, keepdims=True) + EPS)
        o_ref[0, :, gcols] = (yg * nw_ref[:, gcols]).astype(BF16)


def _ssd_out(xc, dt, proj, pf, pb, alog, dtb, dskip, nw, layer):
    b, s, _ = xc.shape
    nc = s // SSD_CHUNK
    chunk = lambda bi, c: (bi, c, 0)
    state = lambda bi, c: (bi, c, 0, 0)
    return pl.pallas_call(
        _ssd_out_kernel,
        grid=(b, nc),
        in_specs=[
            pl.BlockSpec((1, SSD_CHUNK, SSD_XBC), chunk),
            pl.BlockSpec((1, SSD_CHUNK, DT_W), chunk),
            pl.BlockSpec((1, SSD_CHUNK, SSD_INNER), lambda bi, c: (bi, c, PROJ_Z // SSD_INNER)),
            pl.BlockSpec((1, 1, SSD_STATE, SSD_INNER), state),
            pl.BlockSpec((1, 1, SSD_STATE, SSD_INNER), state),
            _wspec(alog, layer, 2),
            _wspec(dtb, layer, 2),
            _wspec(dskip, layer, 2),
            _wspec(nw, layer, 2),
        ],
        out_specs=pl.BlockSpec((1, SSD_CHUNK, SSD_INNER), chunk),
        out_shape=jax.ShapeDtypeStruct((b, s, SSD_INNER), BF16),
        compiler_params=_cparams(("parallel", "parallel"), 32),
        name="ssd_out",
    )(xc, dt, proj, pf, pb, alog, dtb, dskip, nw)


def _out_proj_kernel(oa_ref, ob_ref, oc_ref, wa_ref, wb_ref, wc_ref, x_ref, g_ref, o_ref):
    y = jnp.dot(oa_ref[0], wa_ref[...], preferred_element_type=F32)
    y = y + jnp.dot(ob_ref[0], wb_ref[...], preferred_element_type=F32)
    y = y + jnp.dot(oc_ref[0], wc_ref[...], preferred_element_type=F32)
    o_ref[0] = x_ref[0] + _rms(y, g_ref[...])


def _out_proj(oa, ob, oc, wa, wb, wc, x, g, layer):
    b, s, d = x.shape
    tm = _tile(s, 512)
    row = lambda bi, i: (bi, i, 0)
    return pl.pallas_call(
        _out_proj_kernel,
        grid=(b, s // tm),
        in_specs=[
            pl.BlockSpec((1, tm, oa.shape[-1]), row),
            pl.BlockSpec((1, tm, ob.shape[-1]), row),
            pl.BlockSpec((1, tm, oc.shape[-1]), row),
            _wspec(wa, layer, 2),
            _wspec(wb, layer, 2),
            _wspec(wc, layer, 2),
            pl.BlockSpec((1, tm, d), row),
            _wspec(g, layer, 2),
        ],
        out_specs=pl.BlockSpec((1, tm, d), row),
        out_shape=jax.ShapeDtypeStruct((b, s, d), F32),
        compiler_params=_cparams(("parallel", "parallel"), 48),
        name="out_proj",
    )(oa, ob, oc, wa, wb, wc, x, g)


def _mem_kv_kernel(m_ref, g_ref, w_ref, o_ref):
    h = _rms(m_ref[0], g_ref[...]).astype(BF16)
    o_ref[0] = jnp.dot(h, w_ref[...], preferred_element_type=F32).astype(BF16)


def _mem_kv(mem, g, w, layer):
    b, n, d = mem.shape
    return pl.pallas_call(
        _mem_kv_kernel,
        grid=(b,),
        in_specs=[
            pl.BlockSpec((1, n, d), lambda bi: (bi, 0, 0)),
            _wspec(g, layer, 1),
            _wspec(w, layer, 1),
        ],
        out_specs=pl.BlockSpec((1, n, w.shape[-1]), lambda bi: (bi, 0, 0)),
        out_shape=jax.ShapeDtypeStruct((b, n, w.shape[-1]), BF16),
        compiler_params=_cparams(("parallel",), 32),
        name="mem_kv",
    )(mem, g, w)


def _xattn_kernel(x_ref, gpre_ref, wq_ref, kv_ref, wo_ref, gpost_ref, o_ref):
    x = x_ref[0]
    h = _rms(x, gpre_ref[...]).astype(BF16)
    q = (jnp.dot(h, wq_ref[...], preferred_element_type=F32) * (X_HEAD_DIM ** -0.5)).astype(BF16)
    outs = []
    for hd in range(X_HEADS):
        lanes = slice(hd * X_HEAD_DIM, (hd + 1) * X_HEAD_DIM)
        kh = kv_ref[0, :, lanes]
        vh = kv_ref[0, :, X_WIDTH + hd * X_HEAD_DIM:X_WIDTH + (hd + 1) * X_HEAD_DIM]
        s = lax.dot_general(q[:, lanes], kh, NT_DIMS, preferred_element_type=F32)
        p = jnp.exp(s - jnp.max(s, axis=-1, keepdims=True))
        inv = 1.0 / jnp.sum(p, axis=-1, keepdims=True)
        outs.append((jnp.dot(p.astype(BF16), vh, preferred_element_type=F32) * inv).astype(BF16))
    o = jnp.concatenate(outs, axis=-1)
    y = jnp.dot(o, wo_ref[...], preferred_element_type=F32)
    o_ref[0] = x + _rms(y, gpost_ref[...])


def _xattn(x, gpre, wq, kv, wo, gpost, layer):
    b, s, d = x.shape
    tm = _tile(s, 512)
    row = lambda bi, i: (bi, i, 0)
    return pl.pallas_call(
        _xattn_kernel,
        grid=(b, s // tm),
        in_specs=[
            pl.BlockSpec((1, tm, d), row),
            _wspec(gpre, layer, 2),
            _wspec(wq, layer, 2),
            pl.BlockSpec((1,) + kv.shape[1:], lambda bi, i: (bi, 0, 0)),
            _wspec(wo, layer, 2),
            _wspec(gpost, layer, 2),
        ],
        out_specs=pl.BlockSpec((1, tm, d), row),
        out_shape=jax.ShapeDtypeStruct((b, s, d), F32),
        compiler_params=_cparams(("parallel", "parallel"), 48),
        name="xattn",
    )(x, gpre, wq, kv, wo, gpost)


def _ffn_up_kernel(xp_ref, xm_ref, xn_ref, g_ref, wg_ref, wu_ref, cw_ref, cb_ref, o_ref, h_scr, g_scr, *, tm):
    i = pl.program_id(1)
    halo = BF16_ROWS

    @pl.when(pl.program_id(2) == 0)
    def _():
        g = g_ref[...]
        first = i == 0
        last = i == pl.num_programs(1) - 1
        h_scr[0:halo] = jnp.where(first, 0.0, _rms(xp_ref[0], g)).astype(BF16)
        h_scr[halo:halo + tm] = _rms(xm_ref[0], g).astype(BF16)
        h_scr[halo + tm:] = jnp.where(last, 0.0, _rms(xn_ref[0], g)).astype(BF16)

    g_scr[...] = jnp.dot(h_scr[...], wg_ref[...], preferred_element_type=F32)
    u = jnp.dot(h_scr[halo:halo + tm], wu_ref[...], preferred_element_type=F32)
    r = FFN_CONV // 2
    conv = jnp.broadcast_to(cb_ref[...], u.shape)
    for k in range(FFN_CONV):
        conv = conv + g_scr[pl.ds(halo - r + k, tm), :] * cw_ref[k:k + 1, :]
    o_ref[0] = (_silu(conv) * u).astype(BF16)


def _ffn_up(x, g, w_up, cw, cb, layer):
    b, s, d = x.shape
    tm = _tile(s, 1024)
    tn = 512
    nj = D_FF // tn
    per = tm // BF16_ROWS
    nhalo = s // BF16_ROWS
    return pl.pallas_call(
        functools.partial(_ffn_up_kernel, tm=tm),
        grid=(b, s // tm, nj),
        in_specs=[
            pl.BlockSpec((1, BF16_ROWS, d), lambda bi, i, j: (bi, jnp.maximum(i * per - 1, 0), 0)),
            pl.BlockSpec((1, tm, d), lambda bi, i, j: (bi, i, 0)),
            pl.BlockSpec((1, BF16_ROWS, d), lambda bi, i, j: (bi, jnp.minimum((i + 1) * per, nhalo - 1), 0)),
            _wspec(g, layer, 3),
            pl.BlockSpec((None, d, tn), lambda bi, i, j: (layer, 0, j)),
            pl.BlockSpec((None, d, tn), lambda bi, i, j: (layer, 0, j + nj)),
            pl.BlockSpec((None, FFN_CONV, tn), lambda bi, i, j: (layer, 0, j)),
            pl.BlockSpec((None, 1, tn), lambda bi, i, j: (layer, 0, j)),
        ],
        out_specs=pl.BlockSpec((1, tm, tn), lambda bi, i, j: (bi, i, j)),
        out_shape=jax.ShapeDtypeStruct((b, s, D_FF), BF16),
        scratch_shapes=[
            pltpu.VMEM((tm + 2 * BF16_ROWS, d), BF16),
            pltpu.VMEM((tm + 2 * BF16_ROWS, tn), F32),
        ],
        compiler_params=_cparams(("parallel", "parallel", "arbitrary"), 48),
        name="ffn_up",
    )(x, x, x, g, w_up, w_up, cw, cb)


def _ffn_down_kernel(a_ref, w_ref, x_ref, g_ref, o_ref, acc_scr):
    k = pl.program_id(2)

    @pl.when(k == 0)
    def _():
        acc_scr[...] = jnp.zeros(acc_scr.shape, F32)

    acc_scr[...] += jnp.dot(a_ref[0], w_ref[...], preferred_element_type=F32)

    @pl.when(k == pl.num_programs(2) - 1)
    def _():
        o_ref[0] = x_ref[0] + _rms(acc_scr[...], g_ref[...])


def _ffn_down(a, w, x, g, layer):
    b, s, d = x.shape
    tm = _tile(s, 1024)
    tk = 512
    return pl.pallas_call(
        _ffn_down_kernel,
        grid=(b, s // tm, D_FF // tk),
        in_specs=[
            pl.BlockSpec((1, tm, tk), lambda bi, i, k: (bi, i, k)),
            pl.BlockSpec((None, tk, d), lambda bi, i, k: (layer, k, 0)),
            pl.BlockSpec((1, tm, d), lambda bi, i, k: (bi, i, 0)),
            _wspec(g, layer, 3),
        ],
        out_specs=pl.BlockSpec((1, tm, d), lambda bi, i, k: (bi, i, 0)),
        out_shape=jax.ShapeDtypeStruct((b, s, d), F32),
        scratch_shapes=[pltpu.VMEM((tm, d), F32)],
        compiler_params=_cparams(("parallel", "parallel", "arbitrary"), 56),
        name="ffn_down",
    )(a, w, x, g)


def _rot_half_cols(w):
    half = w.shape[-1] // 2
    return jnp.concatenate([-w[..., half:], w[..., :half]], axis=-1)


def _swa_head_perm():
    half = SWA_HEADS // 2
    order = []
    for p in range(half):
        order += [p, p + half]
    return np.concatenate([np.arange(h * SWA_HEAD_DIM, (h + 1) * SWA_HEAD_DIM) for h in order])


def _pad_lanes(v, width):
    return jnp.pad(v, [(0, 0)] * (v.ndim - 1) + [(0, width - v.shape[-1])])


def _prepare_params(p):
    cuts = np.cumsum([0, MLA_Q_RANK, MLA_KV_RANK, MLA_ROPE, SWA_Q, SWA_KV, SWA_KV, SSD_INNER, SSD_XBC, 2 * SSD_HEADS])
    w_in = p['w_in']
    part = lambda k: w_in[:, :, cuts[k]:cuts[k + 1]]
    cq, ckv, kr, q_s, k_s, v_s, z, xbc, dtw = (part(k) for k in range(9))
    perm = _swa_head_perm()
    w_main = jnp.concatenate([xbc, cq, ckv, kr, _rot_half_cols(kr), q_s[:, :, perm], k_s, v_s, z], axis=-1)
    assert w_main.shape[-1] == PROJ_W

    uq = p['mla_w_uq'].reshape(DEPTH, MLA_Q_RANK, MLA_HEADS, MLA_NOPE + MLA_ROPE)
    uq_rope = uq[..., MLA_NOPE:]
    uq = jnp.concatenate([uq[..., :MLA_NOPE], uq_rope, _rot_half_cols(uq_rope)], axis=-1)
    uq = uq.reshape(DEPTH, MLA_Q_RANK, MLA_HEADS * MLA_QK)
    ukv = p['mla_w_ukv'].reshape(DEPTH, MLA_KV_RANK, MLA_HEADS, MLA_NOPE + MLA_V)
    uk = ukv[..., :MLA_NOPE].reshape(DEPTH, MLA_KV_RANK, MLA_HEADS * MLA_NOPE)
    uv = ukv[..., MLA_NOPE:].reshape(DEPTH, MLA_KV_RANK, MLA_HEADS * MLA_V)

    w_out = p['w_out']
    a_w = MLA_HEADS * MLA_V
    row = lambda v: v[:, None, :].astype(F32)
    return dict(
        norm_mix_pre=row(p['norm_mix_pre']),
        w_main=w_main.astype(BF16),
        w_dt=_pad_lanes(dtw, DT_W).astype(BF16),
        mla_q_norm=row(p['mla_q_norm']),
        mla_kv_norm=row(p['mla_kv_norm']),
        mla_w_uqt=jnp.swapaxes(uq, 1, 2).astype(BF16),
        mla_w_uk=uk.astype(BF16),
        mla_w_uvt=jnp.swapaxes(uv, 1, 2).astype(BF16),
        swa_sink=p['swa_sink'].astype(F32),
        ssd_conv_w=p['ssd_conv_w'].astype(F32),
        ssd_conv_b=row(p['ssd_conv_b']),
        ssd_a_log=_pad_lanes(p['ssd_a_log'].reshape(DEPTH, 1, 2 * SSD_HEADS).astype(F32), DT_W),
        ssd_dt_bias=_pad_lanes(p['ssd_dt_bias'].reshape(DEPTH, 1, 2 * SSD_HEADS).astype(F32), DT_W),
        ssd_d=jnp.repeat(p['ssd_d'].astype(F32), SSD_HEAD_DIM, axis=-1)[:, None, :],
        ssd_norm=row(p['ssd_norm']),
        w_out_a=w_out[:, :a_w].astype(BF16),
        w_out_b=w_out[:, a_w:a_w + SWA_Q][:, perm].astype(BF16),
        w_out_c=w_out[:, a_w + SWA_Q:].astype(BF16),
        norm_mix_post=row(p['norm_mix_post']),
        norm_mem_pre=row(p['norm_mem_pre']),
        norm_mem_post=row(p['norm_mem_post']),
        norm_mem_kv=row(p['norm_mem_kv']),
        xattn_wq=p['xattn_wq'].astype(BF16),
        xattn_wkv=p['xattn_wkv'].astype(BF16),
        xattn_wo=p['xattn_wo'].astype(BF16),
        norm_ffn_pre=row(p['norm_ffn_pre']),
        norm_ffn_post=row(p['norm_ffn_post']),
        ffn_w_up=p['ffn_w_up'].astype(BF16),
        ffn_conv_w=p['ffn_conv_w'].astype(F32),
        ffn_conv_b=row(p['ffn_conv_b']),
        ffn_w_down=p['ffn_w_down'].astype(BF16),
    )


def _rope_table(s):
    half = MLA_ROPE // 2
    inv = ROPE_BASE ** (-jnp.arange(half, dtype=F32) / half)
    ang = jnp.arange(s).astype(F32)[:, None] * inv[None, :]
    cos, sin = jnp.cos(ang), jnp.sin(ang)
    return jnp.concatenate([cos, cos, sin, sin], axis=-1)


def _layer(x, mem, cs, cst, w, layer):
    proj, dt = _in_proj(x, w['norm_mix_pre'], w['w_main'], w['w_dt'], layer)
    qt, k, vt = _mla_prep(proj, cs, cst, w['mla_q_norm'], w['mla_kv_norm'],
                          w['mla_w_uqt'], w['mla_w_uk'], w['mla_w_uvt'], layer)
    o_a = _mla_flash(qt, k, vt)
    o_b = _swa(proj, w['swa_sink'], layer)
    xc, sf, sb, dec = _ssd_local(proj, dt, w['ssd_conv_w'], w['ssd_conv_b'], w['ssd_a_log'], w['ssd_dt_bias'], layer)
    pf, pb = _ssd_scan(sf, sb, dec)
    o_c = _ssd_out(xc, dt, proj, pf, pb, w['ssd_a_log'], w['ssd_dt_bias'], w['ssd_d'], w['ssd_norm'], layer)
    x = _out_proj(o_a, o_b, o_c, w['w_out_a'], w['w_out_b'], w['w_out_c'], x, w['norm_mix_post'], layer)
    kv = _mem_kv(mem, w['norm_mem_kv'], w['xattn_wkv'], layer)
    x = _xattn(x, w['norm_mem_pre'], w['xattn_wq'], kv, w['xattn_wo'], w['norm_mem_post'], layer)
    a = _ffn_up(x, w['norm_ffn_pre'], w['ffn_w_up'], w['ffn_conv_w'], w['ffn_conv_b'], layer)
    return _ffn_down(a, w['ffn_w_down'], x, w['norm_ffn_post'], layer)


def _trunk(x, mem, weights):
    cs = _rope_table(x.shape[1])
    cst = cs.T
    for layer in range(DEPTH):
        x = _layer(x, mem, cs, cst, weights, layer)
    return x


def kernel(x_prompt, x_sample, mem_prompt, mem_sample, norm_mix_pre, norm_mix_post, w_in, mla_q_norm, mla_kv_norm, mla_w_uq, mla_w_ukv, swa_sink, ssd_conv_w, ssd_conv_b, ssd_a_log, ssd_dt_bias, ssd_d, ssd_norm, w_out, norm_mem_pre, norm_mem_post, norm_mem_kv, xattn_wq, xattn_wkv, xattn_wo, norm_ffn_pre, norm_ffn_post, ffn_w_up, ffn_conv_w, ffn_conv_b, ffn_w_down):
    weights = _prepare_params(dict(
        norm_mix_pre=norm_mix_pre, norm_mix_post=norm_mix_post, w_in=w_in,
        mla_q_norm=mla_q_norm, mla_kv_norm=mla_kv_norm, mla_w_uq=mla_w_uq, mla_w_ukv=mla_w_ukv,
        swa_sink=swa_sink, ssd_conv_w=ssd_conv_w, ssd_conv_b=ssd_conv_b, ssd_a_log=ssd_a_log,
        ssd_dt_bias=ssd_dt_bias, ssd_d=ssd_d, ssd_norm=ssd_norm, w_out=w_out,
        norm_mem_pre=norm_mem_pre, norm_mem_post=norm_mem_post, norm_mem_kv=norm_mem_kv,
        xattn_wq=xattn_wq, xattn_wkv=xattn_wkv, xattn_wo=xattn_wo,
        norm_ffn_pre=norm_ffn_pre, norm_ffn_post=norm_ffn_post, ffn_w_up=ffn_w_up,
        ffn_conv_w=ffn_conv_w, ffn_conv_b=ffn_conv_b, ffn_w_down=ffn_w_down))
    return (_trunk(x_prompt, mem_prompt, weights), _trunk(x_sample, mem_sample, weights))
```

```python
import functools

import numpy as np
import jax
import jax.numpy as jnp
from jax import lax
from jax.experimental import pallas as pl
from jax.experimental.pallas import tpu as pltpu

F32 = jnp.float32
BF16 = jnp.bfloat16

D_MODEL = 2048
DEPTH = 4
N_MEM = 256
BLOCK = 128
EPS = 1e-6

MLA_HEADS = 4
MLA_Q_RANK = 384
MLA_KV_RANK = 256
MLA_NOPE = 128
MLA_ROPE = 64
MLA_V = 128
ROPE_BASE = 10000.0
MLA_QK = 2 * MLA_NOPE

SWA_HEADS = 8
SWA_KV_HEADS = 2
SWA_HEAD_DIM = 64
SWA_Q = SWA_HEADS * SWA_HEAD_DIM
SWA_KV = SWA_KV_HEADS * SWA_HEAD_DIM
WINDOW = 128

SSD_HEADS = 16
SSD_HEAD_DIM = 64
SSD_GROUPS = 2
SSD_STATE = 128
SSD_CONV = 5
SSD_CHUNK = 128
SSD_INNER = SSD_HEADS * SSD_HEAD_DIM
SSD_XBC = SSD_INNER + 2 * SSD_GROUPS * SSD_STATE
SSD_GROUP_W = SSD_INNER // SSD_GROUPS
SSD_PAIRS = SSD_HEADS // 2

X_HEADS = 4
X_HEAD_DIM = 128
X_WIDTH = X_HEADS * X_HEAD_DIM

D_FF = 5632
FFN_CONV = 3

LANES = 128
BF16_ROWS = 16
VMEM_MIB = 2 ** 20

PROJ_XBC = 0
PROJ_MLA = SSD_XBC
MLA_SLAB = MLA_Q_RANK + MLA_KV_RANK + 2 * MLA_ROPE
PROJ_SWA = PROJ_MLA + MLA_SLAB
SWA_SLAB = SWA_Q + 2 * SWA_KV
PROJ_Z = PROJ_SWA + SWA_SLAB
PROJ_W = PROJ_Z + SSD_INNER
DT_W = LANES

MLA_VT_ROWS = MLA_V + BF16_ROWS
MLA_TK = 512
MLA_TQ = 2048
MLA_SUB = 512

NT_DIMS = (((1,), (1,)), ((), ()))


def _cparams(semantics, vmem_mib):
    return pltpu.CompilerParams(dimension_semantics=semantics, vmem_limit_bytes=vmem_mib * VMEM_MIB)


def _rms(x, g):
    ms = jnp.mean(x * x, axis=-1, keepdims=True)
    return x * lax.rsqrt(ms + EPS) * g


def _silu(x):
    return x * jax.nn.sigmoid(x)


def _softplus(x):
    return jnp.maximum(x, 0.0) + jnp.log1p(jnp.exp(-jnp.abs(x)))


def _tile(n, pref):
    return pref if n % pref == 0 else n


def _wspec(arr, layer, rank):
    tail = arr.shape[1:]
    idx = (layer,) + (0,) * len(tail)
    if rank == 1:
        return pl.BlockSpec((None,) + tail, lambda a: idx)
    if rank == 2:
        return pl.BlockSpec((None,) + tail, lambda a, b: idx)
    return pl.BlockSpec((None,) + tail, lambda a, b, c: idx)


def _in_proj_kernel(x_ref, g_ref, w_ref, wdt_ref, o_ref, dt_ref, h_scr):
    @pl.when(pl.program_id(2) == 0)
    def _():
        h = _rms(x_ref[0], g_ref[...]).astype(BF16)
        h_scr[...] = h
        dt_ref[0] = jnp.dot(h, wdt_ref[...], preferred_element_type=F32)

    o_ref[0] = jnp.dot(h_scr[...], w_ref[...], preferred_element_type=F32).astype(BF16)


def _in_proj(x, g, w, wdt, layer):
    b, s, d = x.shape
    tm = _tile(s, 1024)
    tn = 1024
    return pl.pallas_call(
        _in_proj_kernel,
        grid=(b, s // tm, PROJ_W // tn),
        in_specs=[
            pl.BlockSpec((1, tm, d), lambda bi, i, j: (bi, i, 0)),
            _wspec(g, layer, 3),
            pl.BlockSpec((None, d, tn), lambda bi, i, j: (layer, 0, j)),
            _wspec(wdt, layer, 3),
        ],
        out_specs=[
            pl.BlockSpec((1, tm, tn), lambda bi, i, j: (bi, i, j)),
            pl.BlockSpec((1, tm, DT_W), lambda bi, i, j: (bi, i, 0)),
        ],
        out_shape=[
            jax.ShapeDtypeStruct((b, s, PROJ_W), BF16),
            jax.ShapeDtypeStruct((b, s, DT_W), F32),
        ],
        scratch_shapes=[pltpu.VMEM((tm, d), BF16)],
        compiler_params=_cparams(("parallel", "parallel", "arbitrary"), 48),
        name="in_proj",
    )(x, g, w, wdt)


def _mla_prep_kernel(a_ref, cs_ref, cst_ref, qn_ref, kvn_ref, wuqt_ref, wuk_ref, wuvt_ref, qt_ref, k_ref, vt_ref):
    a = a_ref[0].astype(F32)
    cq = a[:, :MLA_Q_RANK]
    ckv = a[:, MLA_Q_RANK:MLA_Q_RANK + MLA_KV_RANK]
    kr = a[:, MLA_Q_RANK + MLA_KV_RANK:]
    scale = (MLA_NOPE + MLA_ROPE) ** -0.5
    cqn = _rms(cq, qn_ref[...]).astype(BF16)
    ckvn = _rms(ckv, kvn_ref[...]).astype(BF16)
    qt = lax.dot_general(wuqt_ref[...], cqn, NT_DIMS, preferred_element_type=F32) * scale
    kn = jnp.dot(ckvn, wuk_ref[...], preferred_element_type=F32)
    vt = lax.dot_general(wuvt_ref[...], ckvn, NT_DIMS, preferred_element_type=F32)
    t = kr * cs_ref[...]
    k_rope = (t + pltpu.roll(t, MLA_ROPE, 1)).astype(BF16)
    cst = cst_ref[...]
    ones = jnp.ones((BF16_ROWS, vt.shape[1]), BF16)
    for h in range(MLA_HEADS):
        lo = h * MLA_QK
        mid = lo + MLA_NOPE
        hi = lo + MLA_QK
        qt_ref[0, lo:mid, :] = qt[lo:mid].astype(BF16)
        qt_ref[0, mid:hi, :] = (qt[mid:hi] * cst).astype(BF16)
        k_ref[0, :, lo:mid] = kn[:, h * MLA_NOPE:(h + 1) * MLA_NOPE].astype(BF16)
        k_ref[0, :, mid:hi] = k_rope
        vt_ref[0, h, 0, :MLA_V, :] = vt[h * MLA_V:(h + 1) * MLA_V].astype(BF16)
        vt_ref[0, h, 0, MLA_V:, :] = ones


def _mla_prep(proj, cs, cst, qn, kvn, wuqt, wuk, wuvt, layer):
    b, s, _ = proj.shape
    tm = _tile(s, MLA_TK)
    return pl.pallas_call(
        _mla_prep_kernel,
        grid=(b, s // tm),
        in_specs=[
            pl.BlockSpec((1, tm, MLA_SLAB), lambda bi, i: (bi, i, PROJ_MLA // MLA_SLAB)),
            pl.BlockSpec((tm, LANES), lambda bi, i: (i, 0)),
            pl.BlockSpec((LANES, tm), lambda bi, i: (0, i)),
            _wspec(qn, layer, 2),
            _wspec(kvn, layer, 2),
            _wspec(wuqt, layer, 2),
            _wspec(wuk, layer, 2),
            _wspec(wuvt, layer, 2),
        ],
        out_specs=[
            pl.BlockSpec((1, MLA_HEADS * MLA_QK, tm), lambda bi, i: (bi, 0, i)),
            pl.BlockSpec((1, tm, MLA_HEADS * MLA_QK), lambda bi, i: (bi, i, 0)),
            pl.BlockSpec((1, MLA_HEADS, 1, MLA_VT_ROWS, tm), lambda bi, i: (bi, 0, i, 0, 0)),
        ],
        out_shape=[
            jax.ShapeDtypeStruct((b, MLA_HEADS * MLA_QK, s), BF16),
            jax.ShapeDtypeStruct((b, s, MLA_HEADS * MLA_QK), BF16),
            jax.ShapeDtypeStruct((b, MLA_HEADS, s // tm, MLA_VT_ROWS, tm), BF16),
        ],
        compiler_params=_cparams(("parallel", "parallel"), 32),
        name="mla_prep",
    )(proj, cs, cst, qn, kvn, wuqt, wuk, wuvt)


def _mla_flash_kernel(qt_ref, k_ref, vt_ref, o_ref, *scr, nsub):
    tk = vt_ref.shape[4]
    nck = vt_ref.shape[2]
    sub = qt_ref.shape[2] // nsub
    m_scrs = scr[:nsub]
    acc_scrs = scr[nsub:2 * nsub]
    s_scrs = scr[2 * nsub:]

    def scores(kc, t):
        k = k_ref[0, pl.ds(pl.multiple_of(kc * tk, tk), tk), :]
        return jnp.dot(k, qt_ref[0, :, t * sub:(t + 1) * sub], preferred_element_type=F32)

    for t in range(nsub):
        m_scrs[t][...] = jnp.full(m_scrs[t].shape, -jnp.inf, F32)
        acc_scrs[t][...] = jnp.zeros(acc_scrs[t].shape, F32)
        s_scrs[t][...] = scores(0, t)

    def body(kc, carry):
        vt = vt_ref[0, 0, kc]
        nxt = jnp.minimum(kc + 1, nck - 1)
        for t in range(nsub):
            st = s_scrs[t][...]
            s_scrs[t][...] = scores(nxt, t)
            m_prev = m_scrs[t][...]
            m_new = jnp.maximum(m_prev, jnp.max(st, axis=0, keepdims=True))
            alpha = jnp.exp(m_prev - m_new)
            p = jnp.exp(st - m_new).astype(BF16)
            m_scrs[t][...] = m_new
            acc_scrs[t][...] = alpha * acc_scrs[t][...] + jnp.dot(vt, p, preferred_element_type=F32)
        return carry

    lax.fori_loop(0, nck, body, 0)
    acc = jnp.concatenate([a[...] for a in acc_scrs], axis=1)
    o = acc[:MLA_V] * (1.0 / acc[MLA_V:MLA_V + 1])
    o_ref[0] = o.T.astype(BF16)


def _mla_flash(qt, k, vt):
    b, s, _ = k.shape
    nck, rows, tk = vt.shape[2:]
    tq = _tile(s, MLA_TQ)
    sub = _tile(tq, MLA_SUB)
    nsub = tq // sub
    return pl.pallas_call(
        functools.partial(_mla_flash_kernel, nsub=nsub),
        grid=(b, MLA_HEADS, s // tq),
        in_specs=[
            pl.BlockSpec((1, MLA_QK, tq), lambda bi, h, i: (bi, h, i)),
            pl.BlockSpec((1, s, MLA_QK), lambda bi, h, i: (bi, 0, h)),
            pl.BlockSpec((1, 1, nck, rows, tk), lambda bi, h, i: (bi, h, 0, 0, 0)),
        ],
        out_specs=pl.BlockSpec((1, tq, MLA_V), lambda bi, h, i: (bi, i, h)),
        out_shape=jax.ShapeDtypeStruct((b, s, MLA_HEADS * MLA_V), BF16),
        scratch_shapes=([pltpu.VMEM((1, sub), F32)] * nsub + [pltpu.VMEM((rows, sub), F32)] * nsub
                        + [pltpu.VMEM((tk, sub), F32)] * nsub),
        compiler_params=_cparams(("parallel", "parallel", "arbitrary"), 48),
        name="mla_flash",
    )(qt, k, vt)


def _swa_kernel(sink_ref, main_ref, prev_ref, next_ref, o_ref, *, seq_len, tq, layer):
    i = pl.program_id(1)
    main = main_ref[0]
    kv = jnp.concatenate([prev_ref[0], main[:, SWA_Q:], next_ref[0]], axis=0)
    nkeys = 3 * BLOCK
    qi = lax.broadcasted_iota(jnp.int32, (BLOCK, nkeys), 0)
    kj = lax.broadcasted_iota(jnp.int32, (BLOCK, nkeys), 1)
    dist = jnp.abs(qi + BLOCK - kj)
    distf = dist.astype(F32)
    band = dist <= WINDOW
    low_lanes = lax.broadcasted_iota(jnp.int32, (BLOCK, LANES), 1) < SWA_HEAD_DIM
    top_rows = lax.broadcasted_iota(jnp.int32, (2 * BLOCK, 1), 0) < BLOCK
    half = SWA_HEADS // 2
    grp_heads = SWA_HEADS // SWA_KV_HEADS
    assert half == grp_heads
    slopes = [2.0 ** (-8.0 * (h + 1) / SWA_HEADS) for h in range(SWA_HEADS)]
    zero = jnp.zeros((), BF16)
    for c in range(tq // BLOCK):
        kpos = i * tq + (c - 1) * BLOCK + kj
        valid = band & (kpos >= 0) & (kpos < seq_len)
        valid2 = jnp.concatenate([valid, valid], axis=0)
        kk = kv[c * BLOCK:c * BLOCK + nkeys, :SWA_KV]
        vv = kv[c * BLOCK:c * BLOCK + nkeys, SWA_KV:]
        outs = []
        for p in range(half):
            qp = main[c * BLOCK:(c + 1) * BLOCK, p * LANES:(p + 1) * LANES]
            q2 = jnp.concatenate([jnp.where(low_lanes, qp, zero), jnp.where(low_lanes, zero, qp)], axis=0)
            sc = lax.dot_general(q2, kk, NT_DIMS, preferred_element_type=F32) * (SWA_HEAD_DIM ** -0.5)
            bias = jnp.concatenate([distf * slopes[p], distf * slopes[p + half]], axis=0)
            sc = jnp.where(valid2, sc - bias, -jnp.inf)
            sink = jnp.where(top_rows, sink_ref[layer, p], sink_ref[layer, p + half])
            m = jnp.maximum(jnp.max(sc, axis=-1, keepdims=True), sink)
            pe = jnp.exp(sc - m)
            den = jnp.sum(pe, axis=-1, keepdims=True) + jnp.exp(sink - m)
            pv = jnp.dot(pe.astype(BF16), vv, preferred_element_type=F32) * (1.0 / den)
            outs.append(jnp.where(low_lanes, pv[:BLOCK], pv[BLOCK:]).astype(BF16))
        o_ref[0, c * BLOCK:(c + 1) * BLOCK, :] = jnp.concatenate(outs, axis=-1)


def _swa(proj, sink, layer):
    b, s, _ = proj.shape
    tq = _tile(s, 512)
    per = tq // BLOCK
    nblk = s // BLOCK
    kv_col = (PROJ_SWA + SWA_Q) // (2 * SWA_KV)
    return pl.pallas_call(
        functools.partial(_swa_kernel, seq_len=s, tq=tq, layer=layer),
        grid=(b, s // tq),
        in_specs=[
            pl.BlockSpec(memory_space=pltpu.SMEM),
            pl.BlockSpec((1, tq, SWA_SLAB), lambda bi, i: (bi, i, PROJ_SWA // SWA_SLAB)),
            pl.BlockSpec((1, BLOCK, 2 * SWA_KV), lambda bi, i: (bi, jnp.maximum(i * per - 1, 0), kv_col)),
            pl.BlockSpec((1, BLOCK, 2 * SWA_KV), lambda bi, i: (bi, jnp.minimum((i + 1) * per, nblk - 1), kv_col)),
        ],
        out_specs=pl.BlockSpec((1, tq, SWA_Q), lambda bi, i: (bi, i, 0)),
        out_shape=jax.ShapeDtypeStruct((b, s, SWA_Q), BF16),
        compiler_params=_cparams(("parallel", "parallel"), 32),
        name="swa",
    )(sink, proj, proj, proj)


def _prefix_sum_rows(x):
    n = x.shape[0]
    row = lax.broadcasted_iota(jnp.int32, x.shape, 0)
    k = 1
    while k < n:
        x = x + jnp.where(row >= k, pltpu.roll(x, k, 0), 0.0)
        k *= 2
    return x


def _suffix_sum_rows(x):
    n = x.shape[0]
    row = lax.broadcasted_iota(jnp.int32, x.shape, 0)
    k = 1
    while k < n:
        x = x + jnp.where(row < n - k, pltpu.roll(x, n - k, 0), 0.0)
        k *= 2
    return x


def _ssd_decay_terms(dt_ref, alog_ref, dtb_ref):
    dt = _softplus(dt_ref[0] + dtb_ref[...])
    a = dt * (-jnp.exp(alog_ref[...]))
    pre = _prefix_sum_rows(a)
    suf = _suffix_sum_rows(a)
    fwd_cols = lax.broadcasted_iota(jnp.int32, a.shape, 1) < SSD_HEADS
    acum = jnp.where(fwd_cols, pre, suf)
    tot = pre[SSD_CHUNK - 1:SSD_CHUNK, :]
    return dt, acum, tot


def _pair_expand(t, p, d, low_lanes):
    c = d * SSD_HEADS + 2 * p
    return jnp.where(low_lanes, t[:, c:c + 1], t[:, c + 1:c + 2])


def _ssd_local_kernel(xp_ref, xm_ref, xn_ref, dt_ref, cw_ref, cb_ref, alog_ref, dtb_ref,
                      xc_ref, sf_ref, sb_ref, dec_ref, ext_scr):
    c = pl.program_id(1)
    nc = pl.num_programs(1)
    halo = BF16_ROWS
    r = SSD_CONV // 2
    ext_scr[0:halo] = jnp.where(c > 0, xp_ref[0].astype(F32), 0.0)
    ext_scr[halo:halo + SSD_CHUNK] = xm_ref[0].astype(F32)
    ext_scr[halo + SSD_CHUNK:] = jnp.where(c < nc - 1, xn_ref[0].astype(F32), 0.0)
    acc = jnp.broadcast_to(cb_ref[...], (SSD_CHUNK, SSD_XBC))
    for i in range(SSD_CONV):
        acc = acc + ext_scr[pl.ds(halo - r + i, SSD_CHUNK), :] * cw_ref[i:i + 1, :]
    xc = _silu(acc)
    xc_ref[0] = xc.astype(BF16)

    dt, acum, tot = _ssd_decay_terms(dt_ref, alog_ref, dtb_ref)
    wt = dt * jnp.exp(tot - acum)
    low_lanes = lax.broadcasted_iota(jnp.int32, (SSD_CHUNK, LANES), 1) < SSD_HEAD_DIM
    bts = [xc[:, SSD_INNER + g * SSD_STATE:SSD_INNER + (g + 1) * SSD_STATE].T.astype(BF16)
           for g in range(SSD_GROUPS)]
    for d, s_ref in enumerate((sf_ref, sb_ref)):
        xw = jnp.concatenate(
            [(xc[:, p * LANES:(p + 1) * LANES] * _pair_expand(wt, p, d, low_lanes)).astype(BF16)
             for p in range(SSD_PAIRS)], axis=-1)
        for g in range(SSD_GROUPS):
            cols = slice(g * SSD_GROUP_W, (g + 1) * SSD_GROUP_W)
            s_ref[0, 0, :, cols] = jnp.dot(bts[g], xw[:, cols], preferred_element_type=F32).astype(BF16)
    dec = jnp.broadcast_to(jnp.exp(tot), (8, LANES))
    low8 = low_lanes[:8]
    for d in range(2):
        for p in range(SSD_PAIRS):
            col = d * SSD_INNER + p * LANES
            dec_ref[0, 0, :, col:col + LANES] = _pair_expand(dec, p, d, low8)


def _ssd_local(proj, dt, cw, cb, alog, dtb, layer):
    b, s, _ = proj.shape
    nc = s // SSD_CHUNK
    per = SSD_CHUNK // BF16_ROWS
    nhalo = s // BF16_ROWS
    return pl.pallas_call(
        _ssd_local_kernel,
        grid=(b, nc),
        in_specs=[
            pl.BlockSpec((1, BF16_ROWS, SSD_XBC), lambda bi, c: (bi, jnp.maximum(c * per - 1, 0), 0)),
            pl.BlockSpec((1, SSD_CHUNK, SSD_XBC), lambda bi, c: (bi, c, 0)),
            pl.BlockSpec((1, BF16_ROWS, SSD_XBC), lambda bi, c: (bi, jnp.minimum((c + 1) * per, nhalo - 1), 0)),
            pl.BlockSpec((1, SSD_CHUNK, DT_W), lambda bi, c: (bi, c, 0)),
            _wspec(cw, layer, 2),
            _wspec(cb, layer, 2),
            _wspec(alog, layer, 2),
            _wspec(dtb, layer, 2),
        ],
        out_specs=[
            pl.BlockSpec((1, SSD_CHUNK, SSD_XBC), lambda bi, c: (bi, c, 0)),
            pl.BlockSpec((1, 1, SSD_STATE, SSD_INNER), lambda bi, c: (bi, c, 0, 0)),
            pl.BlockSpec((1, 1, SSD_STATE, SSD_INNER), lambda bi, c: (bi, c, 0, 0)),
            pl.BlockSpec((1, 1, 8, 2 * SSD_INNER), lambda bi, c: (bi, c, 0, 0)),
        ],
        out_shape=[
            jax.ShapeDtypeStruct((b, s, SSD_XBC), BF16),
            jax.ShapeDtypeStruct((b, nc, SSD_STATE, SSD_INNER), BF16),
            jax.ShapeDtypeStruct((b, nc, SSD_STATE, SSD_INNER), BF16),
            jax.ShapeDtypeStruct((b, nc, 8, 2 * SSD_INNER), F32),
        ],
        scratch_shapes=[pltpu.VMEM((SSD_CHUNK + 2 * BF16_ROWS, SSD_XBC), F32)],
        compiler_params=_cparams(("parallel", "parallel"), 32),
        name="ssd_local",
    )(proj, proj, proj, dt, cw, cb, alog, dtb)


def _ssd_scan_kernel(sf_ref, sb_ref, decf_ref, decb_ref, pf_ref, pb_ref, cf_scr, cb_scr):
    @pl.when(pl.program_id(1) == 0)
    def _():
        cf_scr[...] = jnp.zeros(cf_scr.shape, F32)
        cb_scr[...] = jnp.zeros(cb_scr.shape, F32)

    pf_ref[0, 0] = cf_scr[...].astype(BF16)
    pb_ref[0, 0] = cb_scr[...].astype(BF16)
    cf_scr[...] = cf_scr[...] * decf_ref[0, 0, 0:1, :] + sf_ref[0, 0].astype(F32)
    cb_scr[...] = cb_scr[...] * decb_ref[0, 0, 0:1, :] + sb_ref[0, 0].astype(F32)


def _ssd_scan(sf, sb, dec):
    b, nc, n, w = sf.shape
    fwd = lambda bi, c: (bi, c, 0, 0)
    bwd = lambda bi, c: (bi, nc - 1 - c, 0, 0)
    return pl.pallas_call(
        _ssd_scan_kernel,
        grid=(b, nc),
        in_specs=[
            pl.BlockSpec((1, 1, n, w), fwd),
            pl.BlockSpec((1, 1, n, w), bwd),
            pl.BlockSpec((1, 1, 8, w), fwd),
            pl.BlockSpec((1, 1, 8, w), lambda bi, c: (bi, nc - 1 - c, 0, 1)),
        ],
        out_specs=[pl.BlockSpec((1, 1, n, w), fwd), pl.BlockSpec((1, 1, n, w), bwd)],
        out_shape=[jax.ShapeDtypeStruct((b, nc, n, w), BF16), jax.ShapeDtypeStruct((b, nc, n, w), BF16)],
        scratch_shapes=[pltpu.VMEM((n, w), F32), pltpu.VMEM((n, w), F32)],
        compiler_params=_cparams(("parallel", "arbitrary"), 32),
        name="ssd_scan",
    )(sf, sb, dec, dec)


def _ssd_out_kernel(xc_ref, dt_ref, z_ref, pf_ref, pb_ref, alog_ref, dtb_ref, dskip_ref, nw_ref, o_ref):
    dt, acum, _ = _ssd_decay_terms(dt_ref, alog_ref, dtb_ref)
    acum_t = acum.T
    dt_t = dt.T
    eac = jnp.exp(acum)
    row = lax.broadcasted_iota(jnp.int32, (SSD_CHUNK, SSD_CHUNK), 0)
    col = lax.broadcasted_iota(jnp.int32, (SSD_CHUNK, SSD_CHUNK), 1)
    masks = (row >= col, row <= col)
    low_lanes = lax.broadcasted_iota(jnp.int32, (SSD_CHUNK, LANES), 1) < SSD_HEAD_DIM
    zero = jnp.zeros((), BF16)
    pairs_per_group = SSD_PAIRS // SSD_GROUPS
    for g in range(SSD_GROUPS):
        gcols = slice(g * SSD_GROUP_W, (g + 1) * SSD_GROUP_W)
        bm = xc_ref[0, :, SSD_INNER + g * SSD_STATE:SSD_INNER + (g + 1) * SSD_STATE]
        cm_lo = SSD_INNER + SSD_GROUPS * SSD_STATE + g * SSD_STATE
        cm = xc_ref[0, :, cm_lo:cm_lo + SSD_STATE]
        cb = lax.dot_general(cm, bm, NT_DIMS, preferred_element_type=F32)
        y_off = (jnp.dot(cm, pf_ref[0, 0, :, gcols], preferred_element_type=F32),
                 jnp.dot(cm, pb_ref[0, 0, :, gcols], preferred_element_type=F32))
        ys = []
        for pp in range(pairs_per_group):
            p = g * pairs_per_group + pp
            mats = []
            for h in (2 * p, 2 * p + 1):
                w = None
                for d in range(2):
                    k = d * SSD_HEADS + h
                    seg = acum[:, k:k + 1] - acum_t[k:k + 1, :]
                    e = jnp.exp(jnp.where(masks[d], seg, -jnp.inf)) * dt_t[k:k + 1, :]
                    w = e if w is None else w + e
                mats.append((cb * w).astype(BF16))
            lhs = jnp.concatenate(mats, axis=-1)
            xs = xc_ref[0, :, p * LANES:(p + 1) * LANES]
            rhs = jnp.concatenate([jnp.where(low_lanes, xs, zero), jnp.where(low_lanes, zero, xs)], axis=0)
            y = jnp.dot(lhs, rhs, preferred_element_type=F32)
            lanes = slice(pp * LANES, (pp + 1) * LANES)
            for d in range(2):
                y = y + y_off[d][:, lanes] * _pair_expand(eac, p, d, low_lanes)
            y = y + xs.astype(F32) * dskip_ref[:, p * LANES:(p + 1) * LANES]
            y = y * _silu(z_ref[0, :, p * LANES:(p + 1) * LANES].astype(F32))
            ys.append(y)
        yg = jnp.concatenate(ys, axis=-1)
        yg = yg * lax.rsqrt(jnp.mean(yg * yg, axis=-1, keepdims=True) + EPS)
        o_ref[0, :, gcols] = (yg * nw_ref[:, gcols]).astype(BF16)


def _ssd_out(xc, dt, proj, pf, pb, alog, dtb, dskip, nw, layer):
    b, s, _ = xc.shape
    nc = s // SSD_CHUNK
    chunk = lambda bi, c: (bi, c, 0)
    state = lambda bi, c: (bi, c, 0, 0)
    return pl.pallas_call(
        _ssd_out_kernel,
        grid=(b, nc),
        in_specs=[
            pl.BlockSpec((1, SSD_CHUNK, SSD_XBC), chunk),
            pl.BlockSpec((1, SSD_CHUNK, DT_W), chunk),
            pl.BlockSpec((1, SSD_CHUNK, SSD_INNER), lambda bi, c: (bi, c, PROJ_Z // SSD_INNER)),
            pl.BlockSpec((1, 1, SSD_STATE, SSD_INNER), state),
            pl.BlockSpec((1, 1, SSD_STATE, SSD_INNER), state),
            _wspec(alog, layer, 2),
            _wspec(dtb, layer, 2),
            _wspec(dskip, layer, 2),
            _wspec(nw, layer, 2),
        ],
        out_specs=pl.BlockSpec((1, SSD_CHUNK, SSD_INNER), chunk),
        out_shape=jax.ShapeDtypeStruct((b, s, SSD_INNER), BF16),
        compiler_params=_cparams(("parallel", "parallel"), 32),
        name="ssd_out",
    )(xc, dt, proj, pf, pb, alog, dtb, dskip, nw)


def _out_proj_kernel(oa_ref, ob_ref, oc_ref, wa_ref, wb_ref, wc_ref, x_ref, g_ref, o_ref):
    y = jnp.dot(oa_ref[0], wa_ref[...], preferred_element_type=F32)
    y = y + jnp.dot(ob_ref[0], wb_ref[...], preferred_element_type=F32)
    y = y + jnp.dot(oc_ref[0], wc_ref[...], preferred_element_type=F32)
    o_ref[0] = x_ref[0] + _rms(y, g_ref[...])


def _out_proj(oa, ob, oc, wa, wb, wc, x, g, layer):
    b, s, d = x.shape
    tm = _tile(s, 512)
    row = lambda bi, i: (bi, i, 0)
    return pl.pallas_call(
        _out_proj_kernel,
        grid=(b, s // tm),
        in_specs=[
            pl.BlockSpec((1, tm, oa.shape[-1]), row),
            pl.BlockSpec((1, tm, ob.shape[-1]), row),
            pl.BlockSpec((1, tm, oc.shape[-1]), row),
            _wspec(wa, layer, 2),
            _wspec(wb, layer, 2),
            _wspec(wc, layer, 2),
            pl.BlockSpec((1, tm, d), row),
            _wspec(g, layer, 2),
        ],
        out_specs=pl.BlockSpec((1, tm, d), row),
        out_shape=jax.ShapeDtypeStruct((b, s, d), F32),
        compiler_params=_cparams(("parallel", "parallel"), 48),
        name="out_proj",
    )(oa, ob, oc, wa, wb, wc, x, g)


def _mem_kv_kernel(m_ref, g_ref, w_ref, o_ref):
    h = _rms(m_ref[0], g_ref[...]).astype(BF16)
    o_ref[0] = jnp.dot(h, w_ref[...], preferred_element_type=F32).astype(BF16)


def _mem_kv(mem, g, w, layer):
    b, n, d = mem.shape
    return pl.pallas_call(
        _mem_kv_kernel,
        grid=(b,),
        in_specs=[
            pl.BlockSpec((1, n, d), lambda bi: (bi, 0, 0)),
            _wspec(g, layer, 1),
            _wspec(w, layer, 1),
        ],
        out_specs=pl.BlockSpec((1, n, w.shape[-1]), lambda bi: (bi, 0, 0)),
        out_shape=jax.ShapeDtypeStruct((b, n, w.shape[-1]), BF16),
        compiler_params=_cparams(("parallel",), 32),
        name="mem_kv",
    )(mem, g, w)


def _xattn_kernel(x_ref, gpre_ref, wq_ref, kv_ref, wo_ref, gpost_ref, o_ref):
    x = x_ref[0]
    h = _rms(x, gpre_ref[...]).astype(BF16)
    q = (jnp.dot(h, wq_ref[...], preferred_element_type=F32) * (X_HEAD_DIM ** -0.5)).astype(BF16)
    outs = []
    for hd in range(X_HEADS):
        lanes = slice(hd * X_HEAD_DIM, (hd + 1) * X_HEAD_DIM)
        kh = kv_ref[0, :, lanes]
        vh = kv_ref[0, :, X_WIDTH + hd * X_HEAD_DIM:X_WIDTH + (hd + 1) * X_HEAD_DIM]
        s = lax.dot_general(q[:, lanes], kh, NT_DIMS, preferred_element_type=F32)
        p = jnp.exp(s - jnp.max(s, axis=-1, keepdims=True))
        inv = 1.0 / jnp.sum(p, axis=-1, keepdims=True)
        outs.append((jnp.dot(p.astype(BF16), vh, preferred_element_type=F32) * inv).astype(BF16))
    o = jnp.concatenate(outs, axis=-1)
    y = jnp.dot(o, wo_ref[...], preferred_element_type=F32)
    o_ref[0] = x + _rms(y, gpost_ref[...])


def _xattn(x, gpre, wq, kv, wo, gpost, layer):
    b, s, d = x.shape
    tm = _tile(s, 512)
    row = lambda bi, i: (bi, i, 0)
    return pl.pallas_call(
        _xattn_kernel,
        grid=(b, s // tm),
        in_specs=[
            pl.BlockSpec((1, tm, d), row),
            _wspec(gpre, layer, 2),
            _wspec(wq, layer, 2),
            pl.BlockSpec((1,) + kv.shape[1:], lambda bi, i: (bi, 0, 0)),
            _wspec(wo, layer, 2),
            _wspec(gpost, layer, 2),
        ],
        out_specs=pl.BlockSpec((1, tm, d), row),
        out_shape=jax.ShapeDtypeStruct((b, s, d), F32),
        compiler_params=_cparams(("parallel", "parallel"), 48),
        name="xattn",
    )(x, gpre, wq, kv, wo, gpost)


def _ffn_up_kernel(xp_ref, xm_ref, xn_ref, g_ref, wg_ref, wu_ref, cw_ref, cb_ref, o_ref, h_scr, g_scr, *, tm):
    i = pl.program_id(1)
    halo = BF16_ROWS

    @pl.when(pl.program_id(2) == 0)
    def _():
        g = g_ref[...]
        first = i == 0
        last = i == pl.num_programs(1) - 1
        h_scr[0:halo] = jnp.where(first, 0.0, _rms(xp_ref[0], g)).astype(BF16)
        h_scr[halo:halo + tm] = _rms(xm_ref[0], g).astype(BF16)
        h_scr[halo + tm:] = jnp.where(last, 0.0, _rms(xn_ref[0], g)).astype(BF16)

    g_scr[...] = jnp.dot(h_scr[...], wg_ref[...], preferred_element_type=F32)
    u = jnp.dot(h_scr[halo:halo + tm], wu_ref[...], preferred_element_type=F32)
    r = FFN_CONV // 2
    conv = jnp.broadcast_to(cb_ref[...], u.shape)
    for k in range(FFN_CONV):
        conv = conv + g_scr[pl.ds(halo - r + k, tm), :] * cw_ref[k:k + 1, :]
    o_ref[0] = (_silu(conv) * u).astype(BF16)


def _ffn_up(x, g, w_up, cw, cb, layer):
    b, s, d = x.shape
    tm = _tile(s, 1024)
    tn = 512
    nj = D_FF // tn
    per = tm // BF16_ROWS
    nhalo = s // BF16_ROWS
    return pl.pallas_call(
        functools.partial(_ffn_up_kernel, tm=tm),
        grid=(b, s // tm, nj),
        in_specs=[
            pl.BlockSpec((1, BF16_ROWS, d), lambda bi, i, j: (bi, jnp.maximum(i * per - 1, 0), 0)),
            pl.BlockSpec((1, tm, d), lambda bi, i, j: (bi, i, 0)),
            pl.BlockSpec((1, BF16_ROWS, d), lambda bi, i, j: (bi, jnp.minimum((i + 1) * per, nhalo - 1), 0)),
            _wspec(g, layer, 3),
            pl.BlockSpec((None, d, tn), lambda bi, i, j: (layer, 0, j)),
            pl.BlockSpec((None, d, tn), lambda bi, i, j: (layer, 0, j + nj)),
            pl.BlockSpec((None, FFN_CONV, tn), lambda bi, i, j: (layer, 0, j)),
            pl.BlockSpec((None, 1, tn), lambda bi, i, j: (layer, 0, j)),
        ],
        out_specs=pl.BlockSpec((1, tm, tn), lambda bi, i, j: (bi, i, j)),
        out_shape=jax.ShapeDtypeStruct((b, s, D_FF), BF16),
        scratch_shapes=[
            pltpu.VMEM((tm + 2 * BF16_ROWS, d), BF16),
            pltpu.VMEM((tm + 2 * BF16_ROWS, tn), F32),
        ],
        compiler_params=_cparams(("parallel", "parallel", "arbitrary"), 48),
        name="ffn_up",
    )(x, x, x, g, w_up, w_up, cw, cb)


def _ffn_down_kernel(a_ref, w_ref, x_ref, g_ref, o_ref, acc_scr):
    k = pl.program_id(2)

    @pl.when(k == 0)
    def _():
        acc_scr[...] = jnp.zeros(acc_scr.shape, F32)

    acc_scr[...] += jnp.dot(a_ref[0], w_ref[...], preferred_element_type=F32)

    @pl.when(k == pl.num_programs(2) - 1)
    def _():
        o_ref[0] = x_ref[0] + _rms(acc_scr[...], g_ref[...])


def _ffn_down(a, w, x, g, layer):
    b, s, d = x.shape
    tm = _tile(s, 1024)
    tk = 512
    return pl.pallas_call(
        _ffn_down_kernel,
        grid=(b, s // tm, D_FF // tk),
        in_specs=[
            pl.BlockSpec((1, tm, tk), lambda bi, i, k: (bi, i, k)),
            pl.BlockSpec((None, tk, d), lambda bi, i, k: (layer, k, 0)),
            pl.BlockSpec((1, tm, d), lambda bi, i, k: (bi, i, 0)),
            _wspec(g, layer, 3),
        ],
        out_specs=pl.BlockSpec((1, tm, d), lambda bi, i, k: (bi, i, 0)),
        out_shape=jax.ShapeDtypeStruct((b, s, d), F32),
        scratch_shapes=[pltpu.VMEM((tm, d), F32)],
        compiler_params=_cparams(("parallel", "parallel", "arbitrary"), 56),
        name="ffn_down",
    )(a, w, x, g)


def _rot_half_cols(w):
    half = w.shape[-1] // 2
    return jnp.concatenate([-w[..., half:], w[..., :half]], axis=-1)


def _swa_head_perm():
    half = SWA_HEADS // 2
    order = []
    for p in range(half):
        order += [p, p + half]
    return np.concatenate([np.arange(h * SWA_HEAD_DIM, (h + 1) * SWA_HEAD_DIM) for h in order])


def _pad_lanes(v, width):
    return jnp.pad(v, [(0, 0)] * (v.ndim - 1) + [(0, width - v.shape[-1])])


def _prepare_params(p):
    cuts = np.cumsum([0, MLA_Q_RANK, MLA_KV_RANK, MLA_ROPE, SWA_Q, SWA_KV, SWA_KV, SSD_INNER, SSD_XBC, 2 * SSD_HEADS])
    w_in = p['w_in']
    part = lambda k: w_in[:, :, cuts[k]:cuts[k + 1]]
    cq, ckv, kr, q_s, k_s, v_s, z, xbc, dtw = (part(k) for k in range(9))
    perm = _swa_head_perm()
    w_main = jnp.concatenate([xbc, cq, ckv, kr, _rot_half_cols(kr), q_s[:, :, perm], k_s, v_s, z], axis=-1)
    assert w_main.shape[-1] == PROJ_W

    uq = p['mla_w_uq'].reshape(DEPTH, MLA_Q_RANK, MLA_HEADS, MLA_NOPE + MLA_ROPE)
    uq_rope = uq[..., MLA_NOPE:]
    uq = jnp.concatenate([uq[..., :MLA_NOPE], uq_rope, _rot_half_cols(uq_rope)], axis=-1)
    uq = uq.reshape(DEPTH, MLA_Q_RANK, MLA_HEADS * MLA_QK)
    ukv = p['mla_w_ukv'].reshape(DEPTH, MLA_KV_RANK, MLA_HEADS, MLA_NOPE + MLA_V)
    uk = ukv[..., :MLA_NOPE].reshape(DEPTH, MLA_KV_RANK, MLA_HEADS * MLA_NOPE)
    uv = ukv[..., MLA_NOPE:].reshape(DEPTH, MLA_KV_RANK, MLA_HEADS * MLA_V)

    w_out = p['w_out']
    a_w = MLA_HEADS * MLA_V
    row = lambda v: v[:, None, :].astype(F32)
    return dict(
        norm_mix_pre=row(p['norm_mix_pre']),
        w_main=w_main.astype(BF16),
        w_dt=_pad_lanes(dtw, DT_W).astype(BF16),
        mla_q_norm=row(p['mla_q_norm']),
        mla_kv_norm=row(p['mla_kv_norm']),
        mla_w_uqt=jnp.swapaxes(uq, 1, 2).astype(BF16),
        mla_w_uk=uk.astype(BF16),
        mla_w_uvt=jnp.swapaxes(uv, 1, 2).astype(BF16),
        swa_sink=p['swa_sink'].astype(F32),
        ssd_conv_w=p['ssd_conv_w'].astype(F32),
        ssd_conv_b=row(p['ssd_conv_b']),
        ssd_a_log=_pad_lanes(p['ssd_a_log'].reshape(DEPTH, 1, 2 * SSD_HEADS).astype(F32), DT_W),
        ssd_dt_bias=_pad_lanes(p['ssd_dt_bias'].reshape(DEPTH, 1, 2 * SSD_HEADS).astype(F32), DT_W),
        ssd_d=jnp.repeat(p['ssd_d'].astype(F32), SSD_HEAD_DIM, axis=-1)[:, None, :],
        ssd_norm=row(p['ssd_norm']),
        w_out_a=w_out[:, :a_w].astype(BF16),
        w_out_b=w_out[:, a_w:a_w + SWA_Q][:, perm].astype(BF16),
        w_out_c=w_out[:, a_w + SWA_Q:].astype(BF16),
        norm_mix_post=row(p['norm_mix_post']),
        norm_mem_pre=row(p['norm_mem_pre']),
        norm_mem_post=row(p['norm_mem_post']),
        norm_mem_kv=row(p['norm_mem_kv']),
        xattn_wq=p['xattn_wq'].astype(BF16),
        xattn_wkv=p['xattn_wkv'].astype(BF16),
        xattn_wo=p['xattn_wo'].astype(BF16),
        norm_ffn_pre=row(p['norm_ffn_pre']),
        norm_ffn_post=row(p['norm_ffn_post']),
        ffn_w_up=p['ffn_w_up'].astype(BF16),
        ffn_conv_w=p['ffn_conv_w'].astype(F32),
        ffn_conv_b=row(p['ffn_conv_b']),
        ffn_w_down=p['ffn_w_down'].astype(BF16),
    )


def _rope_table(s):
    half = MLA_ROPE // 2
    inv = ROPE_BASE ** (-jnp.arange(half, dtype=F32) / half)
    ang = jnp.arange(s).astype(F32)[:, None] * inv[None, :]
    cos, sin = jnp.cos(ang), jnp.sin(ang)
    return jnp.concatenate([cos, cos, sin, sin], axis=-1)


def _layer(x, mem, cs, cst, w, layer):
    proj, dt = _in_proj(x, w['norm_mix_pre'], w['w_main'], w['w_dt'], layer)
    qt, k, vt = _mla_prep(proj, cs, cst, w['mla_q_norm'], w['mla_kv_norm'],
                          w['mla_w_uqt'], w['mla_w_uk'], w['mla_w_uvt'], layer)
    o_a = _mla_flash(qt, k, vt)
    o_b = _swa(proj, w['swa_sink'], layer)
    xc, sf, sb, dec = _ssd_local(proj, dt, w['ssd_conv_w'], w['ssd_conv_b'], w['ssd_a_log'], w['ssd_dt_bias'], layer)
    pf, pb = _ssd_scan(sf, sb, dec)
    o_c = _ssd_out(xc, dt, proj, pf, pb, w['ssd_a_log'], w['ssd_dt_bias'], w['ssd_d'], w['ssd_norm'], layer)
    x = _out_proj(o_a, o_b, o_c, w['w_out_a'], w['w_out_b'], w['w_out_c'], x, w['norm_mix_post'], layer)
    kv = _mem_kv(mem, w['norm_mem_kv'], w['xattn_wkv'], layer)
    x = _xattn(x, w['norm_mem_pre'], w['xattn_wq'], kv, w['xattn_wo'], w['norm_mem_post'], layer)
    a = _ffn_up(x, w['norm_ffn_pre'], w['ffn_w_up'], w['ffn_conv_w'], w['ffn_conv_b'], layer)
    return _ffn_down(a, w['ffn_w_down'], x, w['norm_ffn_post'], layer)


def _trunk(x, mem, weights):
    cs = _rope_table(x.shape[1])
    cst = cs.T
    for layer in range(DEPTH):
        x = _layer(x, mem, cs, cst, weights, layer)
    return x


def kernel(x_prompt, x_sample, mem_prompt, mem_sample, norm_mix_pre, norm_mix_post, w_in, mla_q_norm, mla_kv_norm, mla_w_uq, mla_w_ukv, swa_sink, ssd_conv_w, ssd_conv_b, ssd_a_log, ssd_dt_bias, ssd_d, ssd_norm, w_out, norm_mem_pre, norm_mem_post, norm_mem_kv, xattn_wq, xattn_wkv, xattn_wo, norm_ffn_pre, norm_ffn_post, ffn_w_up, ffn_conv_w, ffn_conv_b, ffn_w_down):
    weights = _prepare_params(dict(
        norm_mix_pre=norm_mix_pre, norm_mix_post=norm_mix_post, w_in=w_in,
        mla_q_norm=mla_q_norm, mla_kv_norm=mla_kv_norm, mla_w_uq=mla_w_uq, mla_w_ukv=mla_w_ukv,
        swa_sink=swa_sink, ssd_conv_w=ssd_conv_w, ssd_conv_b=ssd_conv_b, ssd_a_log=ssd_a_log,
        ssd_dt_bias=ssd_dt_bias, ssd_d=ssd_d, ssd_norm=ssd_norm, w_out=w_out,
        norm_mem_pre=norm_mem_pre, norm_mem_post=norm_mem_post, norm_mem_kv=norm_mem_kv,
        xattn_wq=xattn_wq, xattn_wkv=xattn_wkv, xattn_wo=xattn_wo,
        norm_ffn_pre=norm_ffn_pre, norm_ffn_post=norm_ffn_post, ffn_w_up=ffn_w_up,
        ffn_conv_w=ffn_conv_w, ffn_conv_b=ffn_conv_b, ffn_w_down=ffn_w_down))
    return (_trunk(x_prompt, mem_prompt, weights), _trunk(x_sample, mem_sample, weights))
```

```python
import functools

import numpy as np
import jax
import jax.numpy as jnp
from jax import lax
from jax.experimental import pallas as pl
from jax.experimental.pallas import tpu as pltpu

F32 = jnp.float32
BF16 = jnp.bfloat16

D_MODEL = 2048
DEPTH = 4
N_MEM = 256
BLOCK = 128
EPS = 1e-6

MLA_HEADS = 4
MLA_Q_RANK = 384
MLA_KV_RANK = 256
MLA_NOPE = 128
MLA_ROPE = 64
MLA_V = 128
ROPE_BASE = 10000.0
MLA_QK = 2 * MLA_NOPE

SWA_HEADS = 8
SWA_KV_HEADS = 2
SWA_HEAD_DIM = 64
SWA_Q = SWA_HEADS * SWA_HEAD_DIM
SWA_KV = SWA_KV_HEADS * SWA_HEAD_DIM
WINDOW = 128

SSD_HEADS = 16
SSD_HEAD_DIM = 64
SSD_GROUPS = 2
SSD_STATE = 128
SSD_CONV = 5
SSD_CHUNK = 128
SSD_INNER = SSD_HEADS * SSD_HEAD_DIM
SSD_XBC = SSD_INNER + 2 * SSD_GROUPS * SSD_STATE
SSD_GROUP_W = SSD_INNER // SSD_GROUPS
SSD_PAIRS = SSD_HEADS // 2

X_HEADS = 4
X_HEAD_DIM = 128
X_WIDTH = X_HEADS * X_HEAD_DIM

D_FF = 5632
FFN_CONV = 3

LANES = 128
BF16_ROWS = 16
VMEM_MIB = 2 ** 20

PROJ_XBC = 0
PROJ_MLA = SSD_XBC
MLA_SLAB = MLA_Q_RANK + MLA_KV_RANK + 2 * MLA_ROPE
PROJ_SWA = PROJ_MLA + MLA_SLAB
SWA_SLAB = SWA_Q + 2 * SWA_KV
PROJ_Z = PROJ_SWA + SWA_SLAB
PROJ_W = PROJ_Z + SSD_INNER
DT_W = LANES

MLA_VT_ROWS = MLA_V + BF16_ROWS
MLA_TK = 512
MLA_TQ = 2048
MLA_SUB = 512

NT_DIMS = (((1,), (1,)), ((), ()))


def _cparams(semantics, vmem_mib):
    return pltpu.CompilerParams(dimension_semantics=semantics, vmem_limit_bytes=vmem_mib * VMEM_MIB)


def _rms(x, g):
    ms = jnp.mean(x * x, axis=-1, keepdims=True)
    return x * lax.rsqrt(ms + EPS) * g


def _silu(x):
    return x * jax.nn.sigmoid(x)


def _softplus(x):
    return jnp.maximum(x, 0.0) + jnp.log1p(jnp.exp(-jnp.abs(x)))


def _tile(n, pref):
    return pref if n % pref == 0 else n


def _wspec(arr, layer, rank):
    tail = arr.shape[1:]
    idx = (layer,) + (0,) * len(tail)
    if rank == 1:
        return pl.BlockSpec((None,) + tail, lambda a: idx)
    if rank == 2:
        return pl.BlockSpec((None,) + tail, lambda a, b: idx)
    return pl.BlockSpec((None,) + tail, lambda a, b, c: idx)


def _in_proj_kernel(x_ref, g_ref, w_ref, wdt_ref, o_ref, dt_ref, h_scr):
    @pl.when(pl.program_id(2) == 0)
    def _():
        h = _rms(x_ref[0], g_ref[...]).astype(BF16)
        h_scr[...] = h
        dt_ref[0] = jnp.dot(h, wdt_ref[...], preferred_element_type=F32)

    o_ref[0] = jnp.dot(h_scr[...], w_ref[...], preferred_element_type=F32).astype(BF16)


def _in_proj(x, g, w, wdt, layer):
    b, s, d = x.shape
    tm = _tile(s, 1024)
    tn = 1024
    return pl.pallas_call(
        _in_proj_kernel,
        grid=(b, s // tm, PROJ_W // tn),
        in_specs=[
            pl.BlockSpec((1, tm, d), lambda bi, i, j: (bi, i, 0)),
            _wspec(g, layer, 3),
            pl.BlockSpec((None, d, tn), lambda bi, i, j: (layer, 0, j)),
            _wspec(wdt, layer, 3),
        ],
        out_specs=[
            pl.BlockSpec((1, tm, tn), lambda bi, i, j: (bi, i, j)),
            pl.BlockSpec((1, tm, DT_W), lambda bi, i, j: (bi, i, 0)),
        ],
        out_shape=[
            jax.ShapeDtypeStruct((b, s, PROJ_W), BF16),
            jax.ShapeDtypeStruct((b, s, DT_W), F32),
        ],
        scratch_shapes=[pltpu.VMEM((tm, d), BF16)],
        compiler_params=_cparams(("parallel", "parallel", "arbitrary"), 48),
        name="in_proj",
    )(x, g, w, wdt)


def _mla_prep_kernel(a_ref, cs_ref, cst_ref, qn_ref, kvn_ref, wuqt_ref, wuk_ref, wuvt_ref, qt_ref, k_ref, vt_ref):
    a = a_ref[0].astype(F32)
    cq = a[:, :MLA_Q_RANK]
    ckv = a[:, MLA_Q_RANK:MLA_Q_RANK + MLA_KV_RANK]
    kr = a[:, MLA_Q_RANK + MLA_KV_RANK:]
    scale = (MLA_NOPE + MLA_ROPE) ** -0.5
    cqn = _rms(cq, qn_ref[...]).astype(BF16)
    ckvn = _rms(ckv, kvn_ref[...]).astype(BF16)
    qt = lax.dot_general(wuqt_ref[...], cqn, NT_DIMS, preferred_element_type=F32) * scale
    kn = jnp.dot(ckvn, wuk_ref[...], preferred_element_type=F32)
    vt = lax.dot_general(wuvt_ref[...], ckvn, NT_DIMS, preferred_element_type=F32)
    t = kr * cs_ref[...]
    k_rope = (t + pltpu.roll(t, MLA_ROPE, 1)).astype(BF16)
    cst = cst_ref[...]
    ones = jnp.ones((BF16_ROWS, vt.shape[1]), BF16)
    for h in range(MLA_HEADS):
        lo = h * MLA_QK
        mid = lo + MLA_NOPE
        hi = lo + MLA_QK
        qt_ref[0, lo:mid, :] = qt[lo:mid].astype(BF16)
        qt_ref[0, mid:hi, :] = (qt[mid:hi] * cst).astype(BF16)
        k_ref[0, :, lo:mid] = kn[:, h * MLA_NOPE:(h + 1) * MLA_NOPE].astype(BF16)
        k_ref[0, :, mid:hi] = k_rope
        vt_ref[0, h, 0, :MLA_V, :] = vt[h * MLA_V:(h + 1) * MLA_V].astype(BF16)
        vt_ref[0, h, 0, MLA_V:, :] = ones


def _mla_prep(proj, cs, cst, qn, kvn, wuqt, wuk, wuvt, layer):
    b, s, _ = proj.shape
    tm = _tile(s, MLA_TK)
    return pl.pallas_call(
        _mla_prep_kernel,
        grid=(b, s // tm),
        in_specs=[
            pl.BlockSpec((1, tm, MLA_SLAB), lambda bi, i: (bi, i, PROJ_MLA // MLA_SLAB)),
            pl.BlockSpec((tm, LANES), lambda bi, i: (i, 0)),
            pl.BlockSpec((LANES, tm), lambda bi, i: (0, i)),
            _wspec(qn, layer, 2),
            _wspec(kvn, layer, 2),
            _wspec(wuqt, layer, 2),
            _wspec(wuk, layer, 2),
            _wspec(wuvt, layer, 2),
        ],
        out_specs=[
            pl.BlockSpec((1, MLA_HEADS * MLA_QK, tm), lambda bi, i: (bi, 0, i)),
            pl.BlockSpec((1, tm, MLA_HEADS * MLA_QK), lambda bi, i: (bi, i, 0)),
            pl.BlockSpec((1, MLA_HEADS, 1, MLA_VT_ROWS, tm), lambda bi, i: (bi, 0, i, 0, 0)),
        ],
        out_shape=[
            jax.ShapeDtypeStruct((b, MLA_HEADS * MLA_QK, s), BF16),
            jax.ShapeDtypeStruct((b, s, MLA_HEADS * MLA_QK), BF16),
            jax.ShapeDtypeStruct((b, MLA_HEADS, s // tm, MLA_VT_ROWS, tm), BF16),
        ],
        compiler_params=_cparams(("parallel", "parallel"), 32),
        name="mla_prep",
    )(proj, cs, cst, qn, kvn, wuqt, wuk, wuvt)


def _mla_flash_kernel(qt_ref, k_ref, vt_ref, o_ref, *scr, nsub):
    tk = vt_ref.shape[4]
    nck = vt_ref.shape[2]
    sub = qt_ref.shape[2] // nsub
    m_scrs = scr[:nsub]
    acc_scrs = scr[nsub:2 * nsub]
    s_scrs = scr[2 * nsub:]

    def scores(kc, t):
        k = k_ref[0, pl.ds(pl.multiple_of(kc * tk, tk), tk), :]
        return jnp.dot(k, qt_ref[0, :, t * sub:(t + 1) * sub], preferred_element_type=F32)

    for t in range(nsub):
        m_scrs[t][...] = jnp.full(m_scrs[t].shape, -jnp.inf, F32)
        acc_scrs[t][...] = jnp.zeros(acc_scrs[t].shape, F32)
        s_scrs[t][...] = scores(0, t)

    def body(kc, carry):
        vt = vt_ref[0, 0, kc]
        nxt = jnp.minimum(kc + 1, nck - 1)
        for t in range(nsub):
            st = s_scrs[t][...]
            s_scrs[t][...] = scores(nxt, t)
            m_prev = m_scrs[t][...]
            m_new = jnp.maximum(m_prev, jnp.max(st, axis=0, keepdims=True))
            alpha = jnp.exp(m_prev - m_new)
            p = jnp.exp(st - m_new).astype(BF16)
            m_scrs[t][...] = m_new
            acc_scrs[t][...] = alpha * acc_scrs[t][...] + jnp.dot(vt, p, preferred_element_type=F32)
        return carry

    lax.fori_loop(0, nck, body, 0)
    acc = jnp.concatenate([a[...] for a in acc_scrs], axis=1)
    o = acc[:MLA_V] * (1.0 / acc[MLA_V:MLA_V + 1])
    o_ref[0] = o.T.astype(BF16)


def _mla_flash(qt, k, vt):
    b, s, _ = k.shape
    nck, rows, tk = vt.shape[2:]
    tq = _tile(s, MLA_TQ)
    sub = _tile(tq, MLA_SUB)
    nsub = tq // sub
    return pl.pallas_call(
        functools.partial(_mla_flash_kernel, nsub=nsub),
        grid=(b, MLA_HEADS, s // tq),
        in_specs=[
            pl.BlockSpec((1, MLA_QK, tq), lambda bi, h, i: (bi, h, i)),
            pl.BlockSpec((1, s, MLA_QK), lambda bi, h, i: (bi, 0, h)),
            pl.BlockSpec((1, 1, nck, rows, tk), lambda bi, h, i: (bi, h, 0, 0, 0)),
        ],
        out_specs=pl.BlockSpec((1, tq, MLA_V), lambda bi, h, i: (bi, i, h)),
        out_shape=jax.ShapeDtypeStruct((b, s, MLA_HEADS * MLA_V), BF16),
        scratch_shapes=([pltpu.VMEM((1, sub), F32)] * nsub + [pltpu.VMEM((rows, sub), F32)] * nsub
                        + [pltpu.VMEM((tk, sub), F32)] * nsub),
        compiler_params=_cparams(("parallel", "parallel", "arbitrary"), 48),
        name="mla_flash",
    )(qt, k, vt)


def _swa_kernel(sink_ref, main_ref, prev_ref, next_ref, o_ref, *, seq_len, tq, layer):
    i = pl.program_id(1)
    main = main_ref[0]
    kv = jnp.concatenate([prev_ref[0], main[:, SWA_Q:], next_ref[0]], axis=0)
    nkeys = 3 * BLOCK
    qi = lax.broadcasted_iota(jnp.int32, (BLOCK, nkeys), 0)
    kj = lax.broadcasted_iota(jnp.int32, (BLOCK, nkeys), 1)
    dist = jnp.abs(qi + BLOCK - kj)
    distf = dist.astype(F32)
    band = dist <= WINDOW
    low_lanes = lax.broadcasted_iota(jnp.int32, (BLOCK, LANES), 1) < SWA_HEAD_DIM
    top_rows = lax.broadcasted_iota(jnp.int32, (2 * BLOCK, 1), 0) < BLOCK
    half = SWA_HEADS // 2
    grp_heads = SWA_HEADS // SWA_KV_HEADS
    assert half == grp_heads
    slopes = [2.0 ** (-8.0 * (h + 1) / SWA_HEADS) for h in range(SWA_HEADS)]
    zero = jnp.zeros((), BF16)
    for c in range(tq // BLOCK):
        kpos = i * tq + (c - 1) * BLOCK + kj
        valid = band & (kpos >= 0) & (kpos < seq_len)
        valid2 = jnp.concatenate([valid, valid], axis=0)
        kk = kv[c * BLOCK:c * BLOCK + nkeys, :SWA_KV]
        vv = kv[c * BLOCK:c * BLOCK + nkeys, SWA_KV:]
        outs = []
        for p in range(half):
            qp = main[c * BLOCK:(c + 1) * BLOCK, p * LANES:(p + 1) * LANES]
            q2 = jnp.concatenate([jnp.where(low_lanes, qp, zero), jnp.where(low_lanes, zero, qp)], axis=0)
            sc = lax.dot_general(q2, kk, NT_DIMS, preferred_element_type=F32) * (SWA_HEAD_DIM ** -0.5)
            bias = jnp.concatenate([distf * slopes[p], distf * slopes[p + half]], axis=0)
            sc = jnp.where(valid2, sc - bias, -jnp.inf)
            sink = jnp.where(top_rows, sink_ref[layer, p], sink_ref[layer, p + half])
            m = jnp.maximum(jnp.max(sc, axis=-1, keepdims=True), sink)
            pe = jnp.exp(sc - m)
            den = jnp.sum(pe, axis=-1, keepdims=True) + jnp.exp(sink - m)
            pv = jnp.dot(pe.astype(BF16), vv, preferred_element_type=F32) * (1.0 / den)
            outs.append(jnp.where(low_lanes, pv[:BLOCK], pv[BLOCK:]).astype(BF16))
        o_ref[0, c * BLOCK:(c + 1) * BLOCK, :] = jnp.concatenate(outs, axis=-1)


def _swa(proj, sink, layer):
    b, s, _ = proj.shape
    tq = _tile(s, 512)
    per = tq // BLOCK
    nblk = s // BLOCK
    kv_col = (PROJ_SWA + SWA_Q) // (2 * SWA_KV)
    return pl.pallas_call(
        functools.partial(_swa_kernel, seq_len=s, tq=tq, layer=layer),
        grid=(b, s // tq),
        in_specs=[
            pl.BlockSpec(memory_space=pltpu.SMEM),
            pl.BlockSpec((1, tq, SWA_SLAB), lambda bi, i: (bi, i, PROJ_SWA // SWA_SLAB)),
            pl.BlockSpec((1, BLOCK, 2 * SWA_KV), lambda bi, i: (bi, jnp.maximum(i * per - 1, 0), kv_col)),
            pl.BlockSpec((1, BLOCK, 2 * SWA_KV), lambda bi, i: (bi, jnp.minimum((i + 1) * per, nblk - 1), kv_col)),
        ],
        out_specs=pl.BlockSpec((1, tq, SWA_Q), lambda bi, i: (bi, i, 0)),
        out_shape=jax.ShapeDtypeStruct((b, s, SWA_Q), BF16),
        compiler_params=_cparams(("parallel", "parallel"), 32),
        name="swa",
    )(sink, proj, proj, proj)


def _prefix_sum_rows(x):
    n = x.shape[0]
    row = lax.broadcasted_iota(jnp.int32, x.shape, 0)
    k = 1
    while k < n:
        x = x + jnp.where(row >= k, pltpu.roll(x, k, 0), 0.0)
        k *= 2
    return x


def _suffix_sum_rows(x):
    n = x.shape[0]
    row = lax.broadcasted_iota(jnp.int32, x.shape, 0)
    k = 1
    while k < n:
        x = x + jnp.where(row < n - k, pltpu.roll(x, n - k, 0), 0.0)
        k *= 2
    return x


def _ssd_decay_terms(dt_ref, alog_ref, dtb_ref):
    dt = _softplus(dt_ref[0] + dtb_ref[...])
    a = dt * (-jnp.exp(alog_ref[...]))
    pre = _prefix_sum_rows(a)
    suf = _suffix_sum_rows(a)
    fwd_cols = lax.broadcasted_iota(jnp.int32, a.shape, 1) < SSD_HEADS
    acum = jnp.where(fwd_cols, pre, suf)
    tot = pre[SSD_CHUNK - 1:SSD_CHUNK, :]
    return dt, acum, tot


def _pair_expand(t, p, d, low_lanes):
    c = d * SSD_HEADS + 2 * p
    return jnp.where(low_lanes, t[:, c:c + 1], t[:, c + 1:c + 2])


def _ssd_local_kernel(xp_ref, xm_ref, xn_ref, dt_ref, cw_ref, cb_ref, alog_ref, dtb_ref,
                      xc_ref, sf_ref, sb_ref, dec_ref, ext_scr):
    c = pl.program_id(1)
    nc = pl.num_programs(1)
    halo = BF16_ROWS
    r = SSD_CONV // 2
    ext_scr[0:halo] = jnp.where(c > 0, xp_ref[0].astype(F32), 0.0)
    ext_scr[halo:halo + SSD_CHUNK] = xm_ref[0].astype(F32)
    ext_scr[halo + SSD_CHUNK:] = jnp.where(c < nc - 1, xn_ref[0].astype(F32), 0.0)
    acc = jnp.broadcast_to(cb_ref[...], (SSD_CHUNK, SSD_XBC))
    for i in range(SSD_CONV):
        acc = acc + ext_scr[pl.ds(halo - r + i, SSD_CHUNK), :] * cw_ref[i:i + 1, :]
    xc = _silu(acc)
    xc_ref[0] = xc.astype(BF16)

    dt, acum, tot = _ssd_decay_terms(dt_ref, alog_ref, dtb_ref)
    wt = dt * jnp.exp(tot - acum)
    low_lanes = lax.broadcasted_iota(jnp.int32, (SSD_CHUNK, LANES), 1) < SSD_HEAD_DIM
    bts = [xc[:, SSD_INNER + g * SSD_STATE:SSD_INNER + (g + 1) * SSD_STATE].T.astype(BF16)
           for g in range(SSD_GROUPS)]
    for d, s_ref in enumerate((sf_ref, sb_ref)):
        xw = jnp.concatenate(
            [(xc[:, p * LANES:(p + 1) * LANES] * _pair_expand(wt, p, d, low_lanes)).astype(BF16)
             for p in range(SSD_PAIRS)], axis=-1)
        for g in range(SSD_GROUPS):
            cols = slice(g * SSD_GROUP_W, (g + 1) * SSD_GROUP_W)
            s_ref[0, 0, :, cols] = jnp.dot(bts[g], xw[:, cols], preferred_element_type=F32)
    dec = jnp.broadcast_to(jnp.exp(tot), (8, LANES))
    low8 = low_lanes[:8]
    for d in range(2):
        for p in range(SSD_PAIRS):
            col = d * SSD_INNER + p * LANES
            dec_ref[0, 0, :, col:col + LANES] = _pair_expand(dec, p, d, low8)


def _ssd_local(proj, dt, cw, cb, alog, dtb, layer):
    b, s, _ = proj.shape
    nc = s // SSD_CHUNK
    per = SSD_CHUNK // BF16_ROWS
    nhalo = s // BF16_ROWS
    return pl.pallas_call(
        _ssd_local_kernel,
        grid=(b, nc),
        in_specs=[
            pl.BlockSpec((1, BF16_ROWS, SSD_XBC), lambda bi, c: (bi, jnp.maximum(c * per - 1, 0), 0)),
            pl.BlockSpec((1, SSD_CHUNK, SSD_XBC), lambda bi, c: (bi, c, 0)),
            pl.BlockSpec((1, BF16_ROWS, SSD_XBC), lambda bi, c: (bi, jnp.minimum((c + 1) * per, nhalo - 1), 0)),
            pl.BlockSpec((1, SSD_CHUNK, DT_W), lambda bi, c: (bi, c, 0)),
            _wspec(cw, layer, 2),
            _wspec(cb, layer, 2),
            _wspec(alog, layer, 2),
            _wspec(dtb, layer, 2),
        ],
        out_specs=[
            pl.BlockSpec((1, SSD_CHUNK, SSD_XBC), lambda bi, c: (bi, c, 0)),
            pl.BlockSpec((1, 1, SSD_STATE, SSD_INNER), lambda bi, c: (bi, c, 0, 0)),
            pl.BlockSpec((1, 1, SSD_STATE, SSD_INNER), lambda bi, c: (bi, c, 0, 0)),
            pl.BlockSpec((1, 1, 8, 2 * SSD_INNER), lambda bi, c: (bi, c, 0, 0)),
        ],
        out_shape=[
            jax.ShapeDtypeStruct((b, s, SSD_XBC), BF16),
            jax.ShapeDtypeStruct((b, nc, SSD_STATE, SSD_INNER), F32),
            jax.ShapeDtypeStruct((b, nc, SSD_STATE, SSD_INNER), F32),
            jax.ShapeDtypeStruct((b, nc, 8, 2 * SSD_INNER), F32),
        ],
        scratch_shapes=[pltpu.VMEM((SSD_CHUNK + 2 * BF16_ROWS, SSD_XBC), F32)],
        compiler_params=_cparams(("parallel", "parallel"), 32),
        name="ssd_local",
    )(proj, proj, proj, dt, cw, cb, alog, dtb)


def _ssd_scan_kernel(sf_ref, sb_ref, decf_ref, decb_ref, pf_ref, pb_ref, cf_scr, cb_scr):
    @pl.when(pl.program_id(1) == 0)
    def _():
        cf_scr[...] = jnp.zeros(cf_scr.shape, F32)
        cb_scr[...] = jnp.zeros(cb_scr.shape, F32)

    pf_ref[0, 0] = cf_scr[...].astype(BF16)
    pb_ref[0, 0] = cb_scr[...].astype(BF16)
    cf_scr[...] = cf_scr[...] * decf_ref[0, 0, 0:1, :] + sf_ref[0, 0]
    cb_scr[...] = cb_scr[...] * decb_ref[0, 0, 0:1, :] + sb_ref[0, 0]


def _ssd_scan(sf, sb, dec):
    b, nc, n, w = sf.shape
    fwd = lambda bi, c: (bi, c, 0, 0)
    bwd = lambda bi, c: (bi, nc - 1 - c, 0, 0)
    return pl.pallas_call(
        _ssd_scan_kernel,
        grid=(b, nc),
        in_specs=[
            pl.BlockSpec((1, 1, n, w), fwd),
            pl.BlockSpec((1, 1, n, w), bwd),
            pl.BlockSpec((1, 1, 8, w), fwd),
            pl.BlockSpec((1, 1, 8, w), lambda bi, c: (bi, nc - 1 - c, 0, 1)),
        ],
        out_specs=[pl.BlockSpec((1, 1, n, w), fwd), pl.BlockSpec((1, 1, n, w), bwd)],
        out_shape=[jax.ShapeDtypeStruct((b, nc, n, w), BF16), jax.ShapeDtypeStruct((b, nc, n, w), BF16)],
        scratch_shapes=[pltpu.VMEM((n, w), F32), pltpu.VMEM((n, w), F32)],
        compiler_params=_cparams(("parallel", "arbitrary"), 32),
        name="ssd_scan",
    )(sf, sb, dec, dec)


def _expand_cols(t, sel_ref):
    live = 2 * SSD_HEADS
    hi = t.astype(BF16)
    r1 = t - hi.astype(F32)
    mid = r1.astype(BF16)
    lo = (r1 - mid.astype(F32)).astype(BF16)
    pad = jnp.zeros((t.shape[0], LANES - 3 * live), BF16)
    packed = jnp.concatenate([hi[:, :live], mid[:, :live], lo[:, :live], pad], axis=1)
    return jnp.dot(packed, sel_ref[...], preferred_element_type=F32)


def _expand_matrix():
    live = 2 * SSD_HEADS
    sel = np.zeros((LANES, live * LANES), np.float32)
    for piece in range(3):
        for k in range(live):
            sel[piece * live + k, k * LANES:(k + 1) * LANES] = 1.0
    return jnp.asarray(sel, BF16)


def _ssd_out_kernel(xc_ref, dt_ref, z_ref, pf_ref, pb_ref, alog_ref, dtb_ref, dskip_ref, nw_ref, sel_ref, o_ref,
                    col_scr):
    dt, acum, _ = _ssd_decay_terms(dt_ref, alog_ref, dtb_ref)
    acum_t = acum.T
    dt_t = dt.T
    col_scr[...] = _expand_cols(acum, sel_ref)
    row = lax.broadcasted_iota(jnp.int32, (SSD_CHUNK, SSD_CHUNK), 0)
    col = lax.broadcasted_iota(jnp.int32, (SSD_CHUNK, SSD_CHUNK), 1)
    masks = (row >= col, row <= col)
    low_lanes = lax.broadcasted_iota(jnp.int32, (SSD_CHUNK, LANES), 1) < SSD_HEAD_DIM
    zero = jnp.zeros((), BF16)
    pairs_per_group = SSD_PAIRS // SSD_GROUPS
    for g in range(SSD_GROUPS):
        gcols = slice(g * SSD_GROUP_W, (g + 1) * SSD_GROUP_W)
        bm = xc_ref[0, :, SSD_INNER + g * SSD_STATE:SSD_INNER + (g + 1) * SSD_STATE]
        cm_lo = SSD_INNER + SSD_GROUPS * SSD_STATE + g * SSD_STATE
        cm = xc_ref[0, :, cm_lo:cm_lo + SSD_STATE]
        cb = lax.dot_general(cm, bm, NT_DIMS, preferred_element_type=F32)
        y_off = (jnp.dot(cm, pf_ref[0, 0, :, gcols], preferred_element_type=F32),
                 jnp.dot(cm, pb_ref[0, 0, :, gcols], preferred_element_type=F32))
        ys = []
        for pp in range(pairs_per_group):
            p = g * pairs_per_group + pp
            mats = []
            for h in (2 * p, 2 * p + 1):
                w = None
                for d in range(2):
                    k = d * SSD_HEADS + h
                    seg = col_scr[:, k * LANES:(k + 1) * LANES] - acum_t[k:k + 1, :]
                    e = jnp.exp(jnp.where(masks[d], seg, -jnp.inf)) * dt_t[k:k + 1, :]
                    w = e if w is None else w + e
                mats.append((cb * w).astype(BF16))
            lhs = jnp.concatenate(mats, axis=-1)
            xs = xc_ref[0, :, p * LANES:(p + 1) * LANES]
            rhs = jnp.concatenate([jnp.where(low_lanes, xs, zero), jnp.where(low_lanes, zero, xs)], axis=0)
            y = jnp.dot(lhs, rhs, preferred_element_type=F32)
            lanes = slice(pp * LANES, (pp + 1) * LANES)
            for d in range(2):
                k = d * SSD_HEADS + 2 * p
                ex = jnp.where(low_lanes, jnp.exp(col_scr[:, k * LANES:(k + 1) * LANES]),
                               jnp.exp(col_scr[:, (k + 1) * LANES:(k + 2) * LANES]))
                y = y + y_off[d][:, lanes] * ex
            y = y + xs.astype(F32) * dskip_ref[:, p * LANES:(p + 1) * LANES]
            y = y * _silu(z_ref[0, :, p * LANES:(p + 1) * LANES].astype(F32))
            ys.append(y)
        yg = jnp.concatenate(ys, axis=-1)
        yg = yg * lax.rsqrt(jnp.mean(yg * yg, axis=-1, keepdims=True) + EPS)
        o_ref[0, :, gcols] = (yg * nw_ref[:, gcols]).astype(BF16)


def _ssd_out(xc, dt, proj, pf, pb, alog, dtb, dskip, nw, layer):
    b, s, _ = xc.shape
    nc = s // SSD_CHUNK
    chunk = lambda bi, c: (bi, c, 0)
    state = lambda bi, c: (bi, c, 0, 0)
    sel = _expand_matrix()
    return pl.pallas_call(
        _ssd_out_kernel,
        grid=(b, nc),
        in_specs=[
            pl.BlockSpec((1, SSD_CHUNK, SSD_XBC), chunk),
            pl.BlockSpec((1, SSD_CHUNK, DT_W), chunk),
            pl.BlockSpec((1, SSD_CHUNK, SSD_INNER), lambda bi, c: (bi, c, PROJ_Z // SSD_INNER)),
            pl.BlockSpec((1, 1, SSD_STATE, SSD_INNER), state),
            pl.BlockSpec((1, 1, SSD_STATE, SSD_INNER), state),
            _wspec(alog, layer, 2),
            _wspec(dtb, layer, 2),
            _wspec(dskip, layer, 2),
            _wspec(nw, layer, 2),
            pl.BlockSpec(sel.shape, lambda bi, c: (0, 0)),
        ],
        out_specs=pl.BlockSpec((1, SSD_CHUNK, SSD_INNER), chunk),
        out_shape=jax.ShapeDtypeStruct((b, s, SSD_INNER), BF16),
        scratch_shapes=[pltpu.VMEM((SSD_CHUNK, sel.shape[1]), F32)],
        compiler_params=_cparams(("parallel", "parallel"), 32),
        name="ssd_out",
    )(xc, dt, proj, pf, pb, alog, dtb, dskip, nw, sel)


def _out_proj_kernel(oa_ref, ob_ref, oc_ref, wa_ref, wb_ref, wc_ref, x_ref, g_ref, o_ref):
    y = jnp.dot(oa_ref[0], wa_ref[...], preferred_element_type=F32)
    y = y + jnp.dot(ob_ref[0], wb_ref[...], preferred_element_type=F32)
    y = y + jnp.dot(oc_ref[0], wc_ref[...], preferred_element_type=F32)
    o_ref[0] = x_ref[0] + _rms(y, g_ref[...])


def _out_proj(oa, ob, oc, wa, wb, wc, x, g, layer):
    b, s, d = x.shape
    tm = _tile(s, 512)
    row = lambda bi, i: (bi, i, 0)
    return pl.pallas_call(
        _out_proj_kernel,
        grid=(b, s // tm),
        in_specs=[
            pl.BlockSpec((1, tm, oa.shape[-1]), row),
            pl.BlockSpec((1, tm, ob.shape[-1]), row),
            pl.BlockSpec((1, tm, oc.shape[-1]), row),
            _wspec(wa, layer, 2),
            _wspec(wb, layer, 2),
            _wspec(wc, layer, 2),
            pl.BlockSpec((1, tm, d), row),
            _wspec(g, layer, 2),
        ],
        out_specs=pl.BlockSpec((1, tm, d), row),
        out_shape=jax.ShapeDtypeStruct((b, s, d), F32),
        compiler_params=_cparams(("parallel", "parallel"), 48),
        name="out_proj",
    )(oa, ob, oc, wa, wb, wc, x, g)


def _mem_kv_kernel(m_ref, g_ref, w_ref, o_ref):
    h = _rms(m_ref[0], g_ref[...]).astype(BF16)
    o_ref[0] = jnp.dot(h, w_ref[...], preferred_element_type=F32).astype(BF16)


def _mem_kv(mem, g, w, layer):
    b, n, d = mem.shape
    return pl.pallas_call(
        _mem_kv_kernel,
        grid=(b,),
        in_specs=[
            pl.BlockSpec((1, n, d), lambda bi: (bi, 0, 0)),
            _wspec(g, layer, 1),
            _wspec(w, layer, 1),
        ],
        out_specs=pl.BlockSpec((1, n, w.shape[-1]), lambda bi: (bi, 0, 0)),
        out_shape=jax.ShapeDtypeStruct((b, n, w.shape[-1]), BF16),
        compiler_params=_cparams(("parallel",), 32),
        name="mem_kv",
    )(mem, g, w)


def _xattn_kernel(x_ref, gpre_ref, wq_ref, kv_ref, wo_ref, gpost_ref, o_ref):
    x = x_ref[0]
    h = _rms(x, gpre_ref[...]).astype(BF16)
    q = (jnp.dot(h, wq_ref[...], preferred_element_type=F32) * (X_HEAD_DIM ** -0.5)).astype(BF16)
    outs = []
    for hd in range(X_HEADS):
        lanes = slice(hd * X_HEAD_DIM, (hd + 1) * X_HEAD_DIM)
        kh = kv_ref[0, :, lanes]
        vh = kv_ref[0, :, X_WIDTH + hd * X_HEAD_DIM:X_WIDTH + (hd + 1) * X_HEAD_DIM]
        s = lax.dot_general(q[:, lanes], kh, NT_DIMS, preferred_element_type=F32)
        p = jnp.exp(s - jnp.max(s, axis=-1, keepdims=True))
        inv = 1.0 / jnp.sum(p, axis=-1, keepdims=True)
        outs.append((jnp.dot(p.astype(BF16), vh, preferred_element_type=F32) * inv).astype(BF16))
    o = jnp.concatenate(outs, axis=-1)
    y = jnp.dot(o, wo_ref[...], preferred_element_type=F32)
    o_ref[0] = x + _rms(y, gpost_ref[...])


def _xattn(x, gpre, wq, kv, wo, gpost, layer):
    b, s, d = x.shape
    tm = _tile(s, 512)
    row = lambda bi, i: (bi, i, 0)
    return pl.pallas_call(
        _xattn_kernel,
        grid=(b, s // tm),
        in_specs=[
            pl.BlockSpec((1, tm, d), row),
            _wspec(gpre, layer, 2),
            _wspec(wq, layer, 2),
            pl.BlockSpec((1,) + kv.shape[1:], lambda bi, i: (bi, 0, 0)),
            _wspec(wo, layer, 2),
            _wspec(gpost, layer, 2),
        ],
        out_specs=pl.BlockSpec((1, tm, d), row),
        out_shape=jax.ShapeDtypeStruct((b, s, d), F32),
        compiler_params=_cparams(("parallel", "parallel"), 48),
        name="xattn",
    )(x, gpre, wq, kv, wo, gpost)


def _ffn_up_kernel(xp_ref, xm_ref, xn_ref, g_ref, wg_ref, wu_ref, cw_ref, cb_ref, o_ref, h_scr, g_scr, *, tm):
    i = pl.program_id(1)
    halo = BF16_ROWS

    @pl.when(pl.program_id(2) == 0)
    def _():
        g = g_ref[...]
        first = i == 0
        last = i == pl.num_programs(1) - 1
        h_scr[0:halo] = jnp.where(first, 0.0, _rms(xp_ref[0], g)).astype(BF16)
        h_scr[halo:halo + tm] = _rms(xm_ref[0], g).astype(BF16)
        h_scr[halo + tm:] = jnp.where(last, 0.0, _rms(xn_ref[0], g)).astype(BF16)

    g_scr[...] = jnp.dot(h_scr[...], wg_ref[...], preferred_element_type=F32)
    u = jnp.dot(h_scr[halo:halo + tm], wu_ref[...], preferred_element_type=F32)
    r = FFN_CONV // 2
    conv = jnp.broadcast_to(cb_ref[...], u.shape)
    for k in range(FFN_CONV):
        conv = conv + g_scr[pl.ds(halo - r + k, tm), :] * cw_ref[k:k + 1, :]
    o_ref[0] = (_silu(conv) * u).astype(BF16)


def _ffn_up(x, g, w_up, cw, cb, layer):
    b, s, d = x.shape
    tm = _tile(s, 1024)
    tn = 512
    nj = D_FF // tn
    per = tm // BF16_ROWS
    nhalo = s // BF16_ROWS
    return pl.pallas_call(
        functools.partial(_ffn_up_kernel, tm=tm),
        grid=(b, s // tm, nj),
        in_specs=[
            pl.BlockSpec((1, BF16_ROWS, d), lambda bi, i, j: (bi, jnp.maximum(i * per - 1, 0), 0)),
            pl.BlockSpec((1, tm, d), lambda bi, i, j: (bi, i, 0)),
            pl.BlockSpec((1, BF16_ROWS, d), lambda bi, i, j: (bi, jnp.minimum((i + 1) * per, nhalo - 1), 0)),
            _wspec(g, layer, 3),
            pl.BlockSpec((None, d, tn), lambda bi, i, j: (layer, 0, j)),
            pl.BlockSpec((None, d, tn), lambda bi, i, j: (layer, 0, j + nj)),
            pl.BlockSpec((None, FFN_CONV, tn), lambda bi, i, j: (layer, 0, j)),
            pl.BlockSpec((None, 1, tn), lambda bi, i, j: (layer, 0, j)),
        ],
        out_specs=pl.BlockSpec((1, tm, tn), lambda bi, i, j: (bi, i, j)),
        out_shape=jax.ShapeDtypeStruct((b, s, D_FF), BF16),
        scratch_shapes=[
            pltpu.VMEM((tm + 2 * BF16_ROWS, d), BF16),
            pltpu.VMEM((tm + 2 * BF16_ROWS, tn), F32),
        ],
        compiler_params=_cparams(("parallel", "parallel", "arbitrary"), 48),
        name="ffn_up",
    )(x, x, x, g, w_up, w_up, cw, cb)


def _ffn_down_kernel(a_ref, w_ref, x_ref, g_ref, o_ref, acc_scr):
    k = pl.program_id(2)

    @pl.when(k == 0)
    def _():
        acc_scr[...] = jnp.zeros(acc_scr.shape, F32)

    acc_scr[...] += jnp.dot(a_ref[0], w_ref[...], preferred_element_type=F32)

    @pl.when(k == pl.num_programs(2) - 1)
    def _():
        o_ref[0] = x_ref[0] + _rms(acc_scr[...], g_ref[...])


def _ffn_down(a, w, x, g, layer):
    b, s, d = x.shape
    tm = _tile(s, 1024)
    tk = 512
    return pl.pallas_call(
        _ffn_down_kernel,
        grid=(b, s // tm, D_FF // tk),
        in_specs=[
            pl.BlockSpec((1, tm, tk), lambda bi, i, k: (bi, i, k)),
            pl.BlockSpec((None, tk, d), lambda bi, i, k: (layer, k, 0)),
            pl.BlockSpec((1, tm, d), lambda bi, i, k: (bi, i, 0)),
            _wspec(g, layer, 3),
        ],
        out_specs=pl.BlockSpec((1, tm, d), lambda bi, i, k: (bi, i, 0)),
        out_shape=jax.ShapeDtypeStruct((b, s, d), F32),
        scratch_shapes=[pltpu.VMEM((tm, d), F32)],
        compiler_params=_cparams(("parallel", "parallel", "arbitrary"), 56),
        name="ffn_down",
    )(a, w, x, g)


def _rot_half_cols(w):
    half = w.shape[-1] // 2
    return jnp.concatenate([-w[..., half:], w[..., :half]], axis=-1)


def _swa_head_perm():
    half = SWA_HEADS // 2
    order = []
    for p in range(half):
        order += [p, p + half]
    return np.concatenate([np.arange(h * SWA_HEAD_DIM, (h + 1) * SWA_HEAD_DIM) for h in order])


def _pad_lanes(v, width):
    return jnp.pad(v, [(0, 0)] * (v.ndim - 1) + [(0, width - v.shape[-1])])


def _prepare_params(p):
    cuts = np.cumsum([0, MLA_Q_RANK, MLA_KV_RANK, MLA_ROPE, SWA_Q, SWA_KV, SWA_KV, SSD_INNER, SSD_XBC, 2 * SSD_HEADS])
    w_in = p['w_in']
    part = lambda k: w_in[:, :, cuts[k]:cuts[k + 1]]
    cq, ckv, kr, q_s, k_s, v_s, z, xbc, dtw = (part(k) for k in range(9))
    perm = _swa_head_perm()
    w_main = jnp.concatenate([xbc, cq, ckv, kr, _rot_half_cols(kr), q_s[:, :, perm], k_s, v_s, z], axis=-1)
    assert w_main.shape[-1] == PROJ_W

    uq = p['mla_w_uq'].reshape(DEPTH, MLA_Q_RANK, MLA_HEADS, MLA_NOPE + MLA_ROPE)
    uq_rope = uq[..., MLA_NOPE:]
    uq = jnp.concatenate([uq[..., :MLA_NOPE], uq_rope, _rot_half_cols(uq_rope)], axis=-1)
    uq = uq.reshape(DEPTH, MLA_Q_RANK, MLA_HEADS * MLA_QK)
    ukv = p['mla_w_ukv'].reshape(DEPTH, MLA_KV_RANK, MLA_HEADS, MLA_NOPE + MLA_V)
    uk = ukv[..., :MLA_NOPE].reshape(DEPTH, MLA_KV_RANK, MLA_HEADS * MLA_NOPE)
    uv = ukv[..., MLA_NOPE:].reshape(DEPTH, MLA_KV_RANK, MLA_HEADS * MLA_V)

    w_out = p['w_out']
    a_w = MLA_HEADS * MLA_V
    row = lambda v: v[:, None, :].astype(F32)
    return dict(
        norm_mix_pre=row(p['norm_mix_pre']),
        w_main=w_main.astype(BF16),
        w_dt=_pad_lanes(dtw, DT_W).astype(BF16),
        mla_q_norm=row(p['mla_q_norm']),
        mla_kv_norm=row(p['mla_kv_norm']),
        mla_w_uqt=jnp.swapaxes(uq, 1, 2).astype(BF16),
        mla_w_uk=uk.astype(BF16),
        mla_w_uvt=jnp.swapaxes(uv, 1, 2).astype(BF16),
        swa_sink=p['swa_sink'].astype(F32),
        ssd_conv_w=p['ssd_conv_w'].astype(F32),
        ssd_conv_b=row(p['ssd_conv_b']),
        ssd_a_log=_pad_lanes(p['ssd_a_log'].reshape(DEPTH, 1, 2 * SSD_HEADS).astype(F32), DT_W),
        ssd_dt_bias=_pad_lanes(p['ssd_dt_bias'].reshape(DEPTH, 1, 2 * SSD_HEADS).astype(F32), DT_W),
        ssd_d=jnp.repeat(p['ssd_d'].astype(F32), SSD_HEAD_DIM, axis=-1)[:, None, :],
        ssd_norm=row(p['ssd_norm']),
        w_out_a=w_out[:, :a_w].astype(BF16),
        w_out_b=w_out[:, a_w:a_w + SWA_Q][:, perm].astype(BF16),
        w_out_c=w_out[:, a_w + SWA_Q:].astype(BF16),
        norm_mix_post=row(p['norm_mix_post']),
        norm_mem_pre=row(p['norm_mem_pre']),
        norm_mem_post=row(p['norm_mem_post']),
        norm_mem_kv=row(p['norm_mem_kv']),
        xattn_wq=p['xattn_wq'].astype(BF16),
        xattn_wkv=p['xattn_wkv'].astype(BF16),
        xattn_wo=p['xattn_wo'].astype(BF16),
        norm_ffn_pre=row(p['norm_ffn_pre']),
        norm_ffn_post=row(p['norm_ffn_post']),
        ffn_w_up=p['ffn_w_up'].astype(BF16),
        ffn_conv_w=p['ffn_conv_w'].astype(F32),
        ffn_conv_b=row(p['ffn_conv_b']),
        ffn_w_down=p['ffn_w_down'].astype(BF16),
    )


def _rope_table(s):
    half = MLA_ROPE // 2
    inv = ROPE_BASE ** (-jnp.arange(half, dtype=F32) / half)
    ang = jnp.arange(s).astype(F32)[:, None] * inv[None, :]
    cos, sin = jnp.cos(ang), jnp.sin(ang)
    return jnp.concatenate([cos, cos, sin, sin], axis=-1)


def _layer(x, mem, cs, cst, w, layer):
    proj, dt = _in_proj(x, w['norm_mix_pre'], w['w_main'], w['w_dt'], layer)
    qt, k, vt = _mla_prep(proj, cs, cst, w['mla_q_norm'], w['mla_kv_norm'],
                          w['mla_w_uqt'], w['mla_w_uk'], w['mla_w_uvt'], layer)
    o_a = _mla_flash(qt, k, vt)
    o_b = _swa(proj, w['swa_sink'], layer)
    xc, sf, sb, dec = _ssd_local(proj, dt, w['ssd_conv_w'], w['ssd_conv_b'], w['ssd_a_log'], w['ssd_dt_bias'], layer)
    pf, pb = _ssd_scan(sf, sb, dec)
    o_c = _ssd_out(xc, dt, proj, pf, pb, w['ssd_a_log'], w['ssd_dt_bias'], w['ssd_d'], w['ssd_norm'], layer)
    x = _out_proj(o_a, o_b, o_c, w['w_out_a'], w['w_out_b'], w['w_out_c'], x, w['norm_mix_post'], layer)
    kv = _mem_kv(mem, w['norm_mem_kv'], w['xattn_wkv'], layer)
    x = _xattn(x, w['norm_mem_pre'], w['xattn_wq'], kv, w['xattn_wo'], w['norm_mem_post'], layer)
    a = _ffn_up(x, w['norm_ffn_pre'], w['ffn_w_up'], w['ffn_conv_w'], w['ffn_conv_b'], layer)
    return _ffn_down(a, w['ffn_w_down'], x, w['norm_ffn_post'], layer)


def _trunk(x, mem, weights):
    cs = _rope_table(x.shape[1])
    cst = cs.T
    for layer in range(DEPTH):
        x = _layer(x, mem, cs, cst, weights, layer)
    return x


def kernel(x_prompt, x_sample, mem_prompt, mem_sample, norm_mix_pre, norm_mix_post, w_in, mla_q_norm, mla_kv_norm, mla_w_uq, mla_w_ukv, swa_sink, ssd_conv_w, ssd_conv_b, ssd_a_log, ssd_dt_bias, ssd_d, ssd_norm, w_out, norm_mem_pre, norm_mem_post, norm_mem_kv, xattn_wq, xattn_wkv, xattn_wo, norm_ffn_pre, norm_ffn_post, ffn_w_up, ffn_conv_w, ffn_conv_b, ffn_w_down):
    weights = _prepare_params(dict(
        norm_mix_pre=norm_mix_pre, norm_mix_post=norm_mix_post, w_in=w_in,
        mla_q_norm=mla_q_norm, mla_kv_norm=mla_kv_norm, mla_w_uq=mla_w_uq, mla_w_ukv=mla_w_ukv,
        swa_sink=swa_sink, ssd_conv_w=ssd_conv_w, ssd_conv_b=ssd_conv_b, ssd_a_log=ssd_a_log,
        ssd_dt_bias=ssd_dt_bias, ssd_d=ssd_d, ssd_norm=ssd_norm, w_out=w_out,
        norm_mem_pre=norm_mem_pre, norm_mem_post=norm_mem_post, norm_mem_kv=norm_mem_kv,
        xattn_wq=xattn_wq, xattn_wkv=xattn_wkv, xattn_wo=xattn_wo,
        norm_ffn_pre=norm_ffn_pre, norm_ffn_post=norm_ffn_post, ffn_w_up=ffn_w_up,
        ffn_conv_w=ffn_conv_w, ffn_conv_b=ffn_conv_b, ffn_w_down=ffn_w_down))
    return (_trunk(x_prompt, mem_prompt, weights), _trunk(x_sample, mem_sample, weights))
```
